```python
import math
import jax, jax.numpy as jnp
from jax import lax
import numpy as np

D_MODEL = 4096
BATCH = 4
SEQ = 2048
DEPTH = 1
DEC_BATCH = 128
DEC_SEQ = 1
PAST_LEN = 8192
PAGE_SIZE = 128

HEAD_DIM = 64
MIX_WIDTH = D_MODEL
WIDTH_A = MIX_WIDTH // 2
WIDTH_B = MIX_WIDTH - WIDTH_A
N_HEADS_A = WIDTH_A // HEAD_DIM
N_KV_A = max(1, N_HEADS_A // 8)
G_A = N_HEADS_A // N_KV_A
N_HEADS_B = WIDTH_B // HEAD_DIM
WINDOW_A = 128
DILATED = ((128, 1), (512, 4), (2048, 16))
WINDOW_B = max(w for w, _ in DILATED)
BLOCK = 128
ROPE_THETA = 10000.0
SCALE = HEAD_DIM ** -0.5
PEER_HEADS = 8
PEER_NKEYS = 128
PEER_EXPERTS = PEER_NKEYS * PEER_NKEYS
PEER_DKEY = 256
PEER_TOPK = 16
PEER_CHUNK = 128
EPS = 1e-6
QKV_COLS = WIDTH_A + 2 * N_KV_A * HEAD_DIM + 3 * WIDTH_B

kernel_name = 'hymba_swa_sink_dilated_peer_step'


def rmsnorm(x, g):
    xf = x.astype(jnp.float32)
    y = xf * lax.rsqrt(jnp.mean(xf * xf, axis=-1, keepdims=True) + EPS)
    return (y * g.astype(jnp.float32)).astype(x.dtype)


def rope(x, pos):
    half = HEAD_DIM // 2
    inv = ROPE_THETA ** (-jnp.arange(half, dtype=jnp.float32) / half)
    ang = pos.astype(jnp.float32)[:, None] * inv[None, :]
    cos = jnp.cos(ang)[:, None, :]
    sin = jnp.sin(ang)[:, None, :]
    xf = x.astype(jnp.float32)
    x1, x2 = xf[..., :half], xf[..., half:]
    return jnp.concatenate([x1 * cos - x2 * sin, x2 * cos + x1 * sin], axis=-1).astype(x.dtype)


def project(n, w_in, pos):
    B, S = n.shape[:2]
    p = jnp.einsum('bsd,dc->bsc', n, w_in)
    cuts = [int(c) for c in np.cumsum([WIDTH_A, N_KV_A * HEAD_DIM, N_KV_A * HEAD_DIM, WIDTH_B, WIDTH_B])]
    qa, ka, va, qb, kb, vb = jnp.split(p, cuts, axis=-1)
    qa = rope(qa.reshape(B, S, N_HEADS_A, HEAD_DIM), pos).reshape(B, S, N_KV_A, G_A, HEAD_DIM)
    ka = rope(ka.reshape(B, S, N_KV_A, HEAD_DIM), pos)
    va = va.reshape(B, S, N_KV_A, HEAD_DIM)
    qb = rope(qb.reshape(B, S, N_HEADS_B, HEAD_DIM), pos)
    kb = rope(kb.reshape(B, S, N_HEADS_B, HEAD_DIM), pos)
    vb = vb.reshape(B, S, N_HEADS_B, HEAD_DIM)
    return qa, ka, va, qb, kb, vb


def masked_softmax(s, valid, sink=None):
    s = jnp.where(valid, s, -jnp.inf)
    m = jnp.max(s, axis=-1, keepdims=True)
    if sink is not None:
        m = jnp.maximum(m, sink)
    p = jnp.exp(s - m)
    den = jnp.sum(p, axis=-1, keepdims=True)
    if sink is not None:
        den = den + jnp.exp(sink - m)
    return p / den, (jnp.log(den) + m)[..., 0]


def band_attention(q, k, v, n_back, sink=None):
    B, L, KV, G, HD = q.shape
    nb = -(-L // BLOCK)
    pad = nb * BLOCK - L
    padl = lambda t: jnp.pad(t, [(0, 0), (0, pad)] + [(0, 0)] * (t.ndim - 2))
    qb = padl(q).reshape(B, nb, BLOCK, KV, G, HD)
    kb = padl(k).reshape(B, nb, BLOCK, KV, HD)
    vb = padl(v).reshape(B, nb, BLOCK, KV, HD)
    with_prev = lambda t: jnp.concatenate(
        [jnp.pad(t[:, :-1], [(0, 0), (1, 0)] + [(0, 0)] * (t.ndim - 2)), t], axis=2)
    kk, vv = with_prev(kb), with_prev(vb)
    s = jnp.einsum('bnqhgd,bnkhd->bnhgqk', qb, kk).astype(jnp.float32) * SCALE
    qi = jnp.arange(BLOCK)[:, None]
    kj = jnp.arange(2 * BLOCK)[None, :]
    dist = qi + BLOCK - kj
    kpos = (jnp.arange(nb) * BLOCK - BLOCK)[:, None, None] + kj[None]
    valid = ((dist >= 0) & (dist <= n_back))[None] & (kpos >= 0)
    sink_b = None if sink is None else sink.astype(jnp.float32)[None, None, :, :, None, None]
    p, lse = masked_softmax(s, valid[None, :, None, None], sink_b)
    o = jnp.einsum('bnhgqk,bnkhd->bnqhgd', p.astype(vv.dtype), vv).reshape(B, nb * BLOCK, KV, G, HD)[:, :L]
    lse = jnp.moveaxis(lse, -1, 2).reshape(B, nb * BLOCK, KV, G)[:, :L]
    return o, lse


def to_sub(t, d):
    B, S = t.shape[:2]
    rest = t.shape[2:]
    t = jnp.swapaxes(t.reshape((B, S // d, d) + rest), 1, 2)
    return t.reshape((B * d, S // d) + rest)


def from_sub(t, B, d):
    L = t.shape[1]
    rest = t.shape[2:]
    t = jnp.swapaxes(t.reshape((B, d, L) + rest), 1, 2)
    return t.reshape((B, L * d) + rest)


def combine_dilations(outs, lses):
    wts = jax.nn.softmax(jnp.stack(lses), axis=0)
    return jnp.einsum('cbsh,cbshd->bshd', wts, jnp.stack(outs))


def dilated_prompt(q, k, v):
    B = q.shape[0]
    outs, lses = [], []
    for w, d in DILATED:
        o, lse = band_attention(to_sub(q[:, :, :, None], d), to_sub(k, d), to_sub(v, d), w // d)
        outs.append(from_sub(o[:, :, :, 0], B, d).astype(jnp.float32))
        lses.append(from_sub(lse[..., 0], B, d))
    return combine_dilations(outs, lses)


def dilated_sample(q, k, v, k_buf, v_buf):
    T = q.shape[1]
    LB = k_buf.shape[1]
    kk = jnp.concatenate([k_buf, k], axis=1)
    vv = jnp.concatenate([v_buf, v], axis=1)
    qidx = LB + jnp.arange(T)
    outs, lses = [], []
    for w, d in DILATED:
        idx = qidx[:, None] - d * jnp.arange(w // d + 1)[None, :]
        valid = idx >= 0
        idx = jnp.maximum(idx, 0)
        kg = jnp.take(kk, idx, axis=1)
        vg = jnp.take(vv, idx, axis=1)
        s = jnp.einsum('bqhd,bqkhd->bhqk', q, kg).astype(jnp.float32) * SCALE
        p, lse = masked_softmax(s, valid[None, None])
        outs.append(jnp.einsum('bhqk,bqkhd->bqhd', p.astype(vg.dtype), vg).astype(jnp.float32))
        lses.append(jnp.swapaxes(lse, 1, 2))
    return combine_dilations(outs, lses)


def window_sample(q, k, v, k_buf, v_buf, sink):
    T = q.shape[1]
    LB = k_buf.shape[1]
    kk = jnp.concatenate([k_buf, k], axis=1)
    vv = jnp.concatenate([v_buf, v], axis=1)
    dist = (LB + jnp.arange(T))[:, None] - jnp.arange(LB + T)[None, :]
    valid = (dist >= 0) & (dist <= WINDOW_A)
    s = jnp.einsum('bqhgd,bkhd->bhgqk', q, kk).astype(jnp.float32) * SCALE
    p, _ = masked_softmax(s, valid[None, None, None], sink.astype(jnp.float32)[None, :, :, None, None])
    return jnp.einsum('bhgqk,bkhd->bqhgd', p.astype(vv.dtype), vv)


def merge_groups(oa, ob, g_out_a, g_out_b, w_out):
    B, S = oa.shape[:2]
    ya = rmsnorm(oa.reshape(B, S, WIDTH_A), g_out_a)
    yb = rmsnorm(ob.reshape(B, S, WIDTH_B), g_out_b).astype(ya.dtype)
    return jnp.einsum('bsc,cd->bsd', jnp.concatenate([ya, yb], axis=-1), w_out)


def peer_ffn(xn, wq, k1, k2, u_tab, v_tab):
    shp = xn.shape
    x2 = xn.reshape(-1, D_MODEL)
    n = x2.shape[0]
    n_chunks = -(-n // PEER_CHUNK)
    x2 = jnp.pad(x2, ((0, n_chunks * PEER_CHUNK - n), (0, 0))).reshape(n_chunks, PEER_CHUNK, D_MODEL)
    half = PEER_DKEY // 2
    k1f = k1.astype(jnp.float32)
    k2f = k2.astype(jnp.float32)

    def chunk(xb):
        q = jnp.einsum('cd,dhk->chk', xb, wq).astype(jnp.float32)
        s1 = jnp.einsum('chk,hnk->chn', q[..., :half], k1f)
        s2 = jnp.einsum('chk,hnk->chn', q[..., half:], k2f)
        v1, i1 = lax.top_k(s1, PEER_TOPK)
        v2, i2 = lax.top_k(s2, PEER_TOPK)
        cand = (v1[..., :, None] + v2[..., None, :]).reshape(PEER_CHUNK, PEER_HEADS, PEER_TOPK * PEER_TOPK)
        cid = (i1[..., :, None] * PEER_NKEYS + i2[..., None, :]).reshape(PEER_CHUNK, PEER_HEADS, PEER_TOPK * PEER_TOPK)
        top_s, top_j = lax.top_k(cand, PEER_TOPK)
        eid = jnp.take_along_axis(cid, top_j, axis=-1).reshape(PEER_CHUNK, PEER_HEADS * PEER_TOPK)
        gate = jax.nn.softmax(top_s, axis=-1).reshape(PEER_CHUNK, PEER_HEADS * PEER_TOPK)
        h = jnp.einsum('cd,ced->ce', xb, u_tab[eid]).astype(jnp.float32)
        a = (jax.nn.gelu(h) * gate).astype(xb.dtype)
        return jnp.einsum('ce,ced->cd', a, v_tab[eid])

    out = lax.map(chunk, x2).reshape(-1, D_MODEL)[:n]
    return out.reshape(shp)


def setup_inputs(seed: int = 0) -> dict:
    key = jax.random.key(seed)
    ks = jax.random.split(key, 20)
    nrm = lambda k, shape, s: jax.random.normal(k, shape, dtype=jnp.float32) * s
    la = min(WINDOW_A, PAST_LEN)
    lb = min(WINDOW_B, PAST_LEN)
    return {
        'x_prompt': nrm(ks[0], (BATCH, SEQ, D_MODEL), 1.0),
        'x_sample': nrm(ks[1], (DEC_BATCH, DEC_SEQ, D_MODEL), 1.0),
        'state_a_k': nrm(ks[2], (DEPTH, DEC_BATCH, la, N_KV_A, HEAD_DIM), 1.0),
        'state_a_v': nrm(ks[3], (DEPTH, DEC_BATCH, la, N_KV_A, HEAD_DIM), 1.0),
        'state_b_k': nrm(ks[4], (DEPTH, DEC_BATCH, lb, N_HEADS_B, HEAD_DIM), 1.0),
        'state_b_v': nrm(ks[5], (DEPTH, DEC_BATCH, lb, N_HEADS_B, HEAD_DIM), 1.0),
        'g_attn': 1.0 + nrm(ks[6], (DEPTH, D_MODEL), 0.02),
        'w_in': nrm(ks[7], (DEPTH, D_MODEL, QKV_COLS), D_MODEL ** -0.5),
        'attn_sink': nrm(ks[8], (DEPTH, N_HEADS_A), 0.5),
        'g_out_a': 1.0 + nrm(ks[9], (DEPTH, WIDTH_A), 0.02),
        'g_out_b': 1.0 + nrm(ks[10], (DEPTH, WIDTH_B), 0.02),
        'w_out': nrm(ks[11], (DEPTH, MIX_WIDTH, D_MODEL), MIX_WIDTH ** -0.5),
        'g_ffn': 1.0 + nrm(ks[12], (DEPTH, D_MODEL), 0.02),
        'peer_wq': nrm(ks[13], (DEPTH, D_MODEL, PEER_HEADS, PEER_DKEY), D_MODEL ** -0.5),
        'peer_k1': nrm(ks[14], (DEPTH, PEER_HEADS, PEER_NKEYS, PEER_DKEY // 2), (PEER_DKEY // 2) ** -0.5),
        'peer_k2': nrm(ks[15], (DEPTH, PEER_HEADS, PEER_NKEYS, PEER_DKEY // 2), (PEER_DKEY // 2) ** -0.5),
        'peer_u': nrm(ks[16], (DEPTH, PEER_EXPERTS, D_MODEL), D_MODEL ** -0.5),
        'peer_v': nrm(ks[17], (DEPTH, PEER_EXPERTS, D_MODEL), 0.5 * PEER_HEADS ** -0.5),
        'g_final': 1.0 + nrm(ks[18], (D_MODEL,), 0.02),
    }


def reference(x_prompt, x_sample, state_a_k, state_a_v, state_b_k, state_b_v, g_attn, w_in, attn_sink,
              g_out_a, g_out_b, w_out, g_ffn, peer_wq, peer_k1, peer_k2, peer_u, peer_v, g_final):
    S = x_prompt.shape[1]
    T = x_sample.shape[1]
    pos_p = jnp.arange(S)
    pos_s = PAST_LEN + jnp.arange(T)
    rows_a = min(WINDOW_A, S)
    rows_b = min(WINDOW_B, S)
    hp, hs = x_prompt, x_sample
    pak, pav, pbk, pbv = [], [], [], []
    sak, sav, sbk, sbv = [], [], [], []
    for l in range(DEPTH):
        sink = attn_sink[l].reshape(N_KV_A, G_A)
        qa, ka, va, qb, kb, vb = project(rmsnorm(hp, g_attn[l]), w_in[l], pos_p)
        oa, _ = band_attention(qa, ka, va, WINDOW_A, sink)
        ob = dilated_prompt(qb, kb, vb)
        hp = hp + merge_groups(oa, ob, g_out_a[l], g_out_b[l], w_out[l])
        hp = hp + peer_ffn(rmsnorm(hp, g_ffn[l]), peer_wq[l], peer_k1[l], peer_k2[l], peer_u[l], peer_v[l])
        pak.append(ka[:, S - rows_a:])
        pav.append(va[:, S - rows_a:])
        pbk.append(kb[:, S - rows_b:])
        pbv.append(vb[:, S - rows_b:])
        qa, ka, va, qb, kb, vb = project(rmsnorm(hs, g_attn[l]), w_in[l], pos_s)
        oa = window_sample(qa, ka, va, state_a_k[l], state_a_v[l], sink)
        ob = dilated_sample(qb, kb, vb, state_b_k[l], state_b_v[l])
        hs = hs + merge_groups(oa, ob, g_out_a[l], g_out_b[l], w_out[l])
        hs = hs + peer_ffn(rmsnorm(hs, g_ffn[l]), peer_wq[l], peer_k1[l], peer_k2[l], peer_u[l], peer_v[l])
        sak.append(ka)
        sav.append(va)
        sbk.append(kb)
        sbv.append(vb)
    y_prompt = rmsnorm(hp, g_final)
    y_sample = rmsnorm(hs, g_final)
    return (y_prompt, y_sample, jnp.stack(pak), jnp.stack(pav), jnp.stack(pbk), jnp.stack(pbv),
            jnp.stack(sak), jnp.stack(sav), jnp.stack(sbk), jnp.stack(sbv))
```

```python
import functools
import math

import jax
import jax.numpy as jnp
import numpy as np
from jax import lax
from jax.experimental import pallas as pl
from jax.experimental.pallas import tpu as pltpu

D_MODEL = 4096
HEAD_DIM = 64
WIDTH_A = 2048
WIDTH_B = 2048
N_HEADS_A = 32
N_KV_A = 4
G_A = 8
N_HEADS_B = 32
WINDOW_A = 128
DILATED = ((128, 1), (512, 4), (2048, 16))
BLOCK = 128
PAST_LEN = 8192
ROPE_THETA = 10000.0
SCALE = HEAD_DIM ** -0.5
PEER_HEADS = 8
PEER_NKEYS = 128
PEER_EXPERTS = PEER_NKEYS * PEER_NKEYS
PEER_DKEY = 256
PEER_TOPK = 16
EPS = 1e-6
QKV_COLS = WIDTH_A + 2 * N_KV_A * HEAD_DIM + 3 * WIDTH_B

COL_QA = 0
COL_KA = WIDTH_A // 128
COL_VA = COL_KA + N_KV_A * HEAD_DIM // 128
COL_QB = COL_VA + N_KV_A * HEAD_DIM // 128
COL_KB = COL_QB + WIDTH_B // 128
COL_VB = COL_KB + WIDTH_B // 128

LANES = 128
VMEM_LIMIT = 56 * 1024 * 1024

BF16 = jnp.bfloat16
F32 = jnp.float32
NEG_INF = float("-inf")


def _cparams(sem):
    return pltpu.CompilerParams(dimension_semantics=sem, vmem_limit_bytes=VMEM_LIMIT)


PROJ_TN = 512


def _proj_kernel(x_ref, g_ref, w_ref, cos_ref, sin_ref, rope_ref, o_ref, xn_ref):
    j = pl.program_id(1)

    @pl.when(j == 0)
    def _():
        x = x_ref[...]
        ms = jnp.mean(x * x, axis=-1, keepdims=True)
        xn_ref[...] = ((x * lax.rsqrt(ms + EPS)) * g_ref[...]).astype(BF16)

    p = jnp.dot(xn_ref[...], w_ref[...], preferred_element_type=F32)
    lane = lax.broadcasted_iota(jnp.int32, (1, LANES), 1)
    first_half = (lane % HEAD_DIM) < (HEAD_DIM // 2)
    cos = cos_ref[...]
    sin = sin_ref[...]
    for c in range(PROJ_TN // LANES):
        sl = slice(c * LANES, (c + 1) * LANES)
        pc = p[:, sl]
        partner = jnp.where(first_half, pltpu.roll(pc, LANES - HEAD_DIM // 2, 1),
                            pltpu.roll(pc, HEAD_DIM // 2, 1))
        roped = pc * cos + partner * sin
        o_ref[:, sl] = jnp.where(rope_ref[:, sl] > 0.0, roped, pc)


def _proj(x2d, g, w_bf, cos_t, sin_t, rope_flags, tm, pos_blocks):
    t = x2d.shape[0]
    grid = (t // tm, QKV_COLS // PROJ_TN)
    return pl.pallas_call(
        _proj_kernel,
        grid=grid,
        in_specs=[
            pl.BlockSpec((tm, D_MODEL), lambda i, j: (i, 0)),
            pl.BlockSpec((1, D_MODEL), lambda i, j: (0, 0)),
            pl.BlockSpec((D_MODEL, PROJ_TN), lambda i, j: (0, j)),
            pl.BlockSpec((tm, LANES), lambda i, j: (i % pos_blocks, 0)),
            pl.BlockSpec((tm, LANES), lambda i, j: (i % pos_blocks, 0)),
            pl.BlockSpec((1, PROJ_TN), lambda i, j: (0, j)),
        ],
        out_specs=pl.BlockSpec((tm, PROJ_TN), lambda i, j: (i, j)),
        out_shape=jax.ShapeDtypeStruct((t, QKV_COLS), F32),
        scratch_shapes=[pltpu.VMEM((tm, D_MODEL), BF16)],
        compiler_params=_cparams(("arbitrary", "arbitrary")),
        name="proj",
    )(x2d, g, w_bf, cos_t, sin_t, rope_flags)


def _rope_tables(pos):
    half = HEAD_DIM // 2
    inv = ROPE_THETA ** (-jnp.arange(half, dtype=F32) / half)
    ang = pos.astype(F32)[:, None] * inv[None, :]
    cos = jnp.cos(ang)
    sin = jnp.sin(ang)
    cos_t = jnp.concatenate([cos, cos, cos, cos], axis=-1)
    sin_t = jnp.concatenate([-sin, sin, -sin, sin], axis=-1)
    return cos_t, sin_t


def _attn_a_kernel(sink_ref, q_ref, kp_ref, kc_ref, vp_ref, vc_ref, o_ref):
    kp_id = pl.program_id(1)
    n = pl.program_id(2)
    lane = lax.broadcasted_iota(jnp.int32, (1, LANES), 1)
    lo = lane < HEAD_DIM
    qi = lax.broadcasted_iota(jnp.int32, (BLOCK, 2 * BLOCK), 0)
    kj = lax.broadcasted_iota(jnp.int32, (BLOCK, 2 * BLOCK), 1)
    first_key = jnp.where(n == 0, BLOCK, 0)
    valid = (kj >= qi) & (kj <= qi + WINDOW_A) & (kj >= first_key)
    k2 = jnp.concatenate([kp_ref[...], kc_ref[...]], axis=0)
    v2 = jnp.concatenate([vp_ref[...], vc_ref[...]], axis=0)
    nt = (((1,), (1,)), ((), ()))
    for kvl in range(2):
        if kvl == 0:
            k_lo = jnp.where(lo, k2, 0.0)
            k_hi = pltpu.roll(k_lo, HEAD_DIM, 1)
            v_lo = jnp.where(lo, v2, 0.0)
            v_hi = pltpu.roll(v_lo, HEAD_DIM, 1)
        else:
            k_hi = jnp.where(lo, 0.0, k2)
            k_lo = pltpu.roll(k_hi, HEAD_DIM, 1)
            v_hi = jnp.where(lo, 0.0, v2)
            v_lo = pltpu.roll(v_hi, HEAD_DIM, 1)
        kb = (k_lo.astype(BF16), k_hi.astype(BF16))
        vb = (v_lo.astype(BF16), v_hi.astype(BF16))
        for c in range(G_A // 2):
            col = (kvl * (G_A // 2) + c) * LANES
            qc = q_ref[:, col:col + LANES].astype(BF16)
            acc = jnp.zeros((BLOCK, LANES), F32)
            for par in range(2):
                h = kp_id * (2 * G_A) + kvl * G_A + c * 2 + par
                sk = sink_ref[h]
                s = lax.dot_general(qc, kb[par], nt, preferred_element_type=F32) * SCALE
                s = jnp.where(valid, s, NEG_INF)
                m = jnp.maximum(jnp.max(s, axis=-1, keepdims=True), sk)
                p = jnp.exp(s - m)
                den = jnp.sum(p, axis=-1, keepdims=True) + jnp.exp(sk - m)
                o = jnp.dot(p.astype(BF16), vb[par], preferred_element_type=F32)
                acc = acc + o * (1.0 / den)
            o_ref[:, col:col + LANES] = acc


def _attn_a(p3, sink):
    b, s, _ = p3.shape
    nb = s // BLOCK
    qw = 2 * G_A * HEAD_DIM
    grid = (b, N_KV_A // 2, nb)
    prev = lambda n: jnp.maximum(n - 1, 0)
    return pl.pallas_call(
        _attn_a_kernel,
        grid=grid,
        in_specs=[
            pl.BlockSpec(memory_space=pltpu.SMEM),
            pl.BlockSpec((None, BLOCK, qw), lambda bi, kp, n: (bi, n, kp)),
            pl.BlockSpec((None, BLOCK, LANES), lambda bi, kp, n: (bi, prev(n), COL_KA + kp)),
            pl.BlockSpec((None, BLOCK, LANES), lambda bi, kp, n: (bi, n, COL_KA + kp)),
            pl.BlockSpec((None, BLOCK, LANES), lambda bi, kp, n: (bi, prev(n), COL_VA + kp)),
            pl.BlockSpec((None, BLOCK, LANES), lambda bi, kp, n: (bi, n, COL_VA + kp)),
        ],
        out_specs=pl.BlockSpec((None, BLOCK, qw), lambda bi, kp, n: (bi, n, kp)),
        out_shape=jax.ShapeDtypeStruct((b, s, WIDTH_A), F32),
        compiler_params=_cparams(("arbitrary", "arbitrary", "arbitrary")),
        name="attn_a",
    )(sink, p3, p3, p3, p3, p3)


def _attn_b_kernel(q_ref, k_ref, v_ref, o_ref, oc_ref, lc_ref):
    seq = q_ref.shape[0]
    lane = lax.broadcasted_iota(jnp.int32, (1, LANES), 1)
    lo = lane < HEAD_DIM
    nt = (((1,), (1,)), ((), ()))

    def block(cfg, d, start, has_prev):
        rows_q = pl.ds(start, BLOCK, stride=d) if d > 1 else pl.ds(start, BLOCK)
        q = q_ref[rows_q, :].astype(BF16)
        nk = 2 * BLOCK if has_prev else BLOCK
        kstart = start - BLOCK * d if has_prev else start
        rows_k = pl.ds(kstart, nk, stride=d) if d > 1 else pl.ds(kstart, nk)
        k = k_ref[rows_k, :]
        v = v_ref[rows_k, :]
        qi = lax.broadcasted_iota(jnp.int32, (BLOCK, nk), 0)
        kj = lax.broadcasted_iota(jnp.int32, (BLOCK, nk), 1)
        if has_prev:
            valid = (kj >= qi) & (kj <= qi + BLOCK)
        else:
            valid = kj <= qi
        acc = jnp.zeros((BLOCK, LANES), F32)
        lse = jnp.zeros((BLOCK, LANES), F32)
        for par in range(2):
            msk = lo if par == 0 else jnp.logical_not(lo)
            kpar = jnp.where(msk, k, 0.0).astype(BF16)
            vpar = jnp.where(msk, v, 0.0).astype(BF16)
            s = lax.dot_general(q, kpar, nt, preferred_element_type=F32) * SCALE
            s = jnp.where(valid, s, NEG_INF)
            m = jnp.max(s, axis=-1, keepdims=True)
            p = jnp.exp(s - m)
            den = jnp.sum(p, axis=-1, keepdims=True)
            o = jnp.dot(p.astype(BF16), vpar, preferred_element_type=F32)
            acc = acc + o * (1.0 / den)
            lse = jnp.where(msk, jnp.log(den) + m, lse)
        oc_ref[cfg, rows_q, :] = acc
        lc_ref[cfg, rows_q, :] = lse

    for cfg, (w, d) in enumerate(DILATED):
        assert w // d == BLOCK
        sub_len = seq // d
        nblk = sub_len // BLOCK

        def per_class(r, carry, cfg=cfg, d=d, nblk=nblk):
            block(cfg, d, r, False)
            if nblk > 1:
                def per_block(nb_i, c2):
                    block(cfg, d, r + nb_i * (BLOCK * d), True)
                    return c2
                lax.fori_loop(1, nblk, per_block, 0)
            return carry

        lax.fori_loop(0, d, per_class, 0)

    def combine(i, carry):
        rows = pl.ds(pl.multiple_of(i * BLOCK, BLOCK), BLOCK)
        l0 = lc_ref[0, rows, :]
        l1 = lc_ref[1, rows, :]
        l2 = lc_ref[2, rows, :]
        mx = jnp.maximum(jnp.maximum(l0, l1), l2)
        w0 = jnp.exp(l0 - mx)
        w1 = jnp.exp(l1 - mx)
        w2 = jnp.exp(l2 - mx)
        num = w0 * oc_ref[0, rows, :] + w1 * oc_ref[1, rows, :] + w2 * oc_ref[2, rows, :]
        o_ref[rows, :] = num / (w0 + w1 + w2)
        return carry

    lax.fori_loop(0, seq // BLOCK, combine, 0)


def _attn_b(p3):
    b, s, _ = p3.shape
    grid = (b, N_HEADS_B // 2)
    ncfg = len(DILATED)
    return pl.pallas_call(
        _attn_b_kernel,
        grid=grid,
        in_specs=[
            pl.BlockSpec((None, s, LANES), lambda bi, hp: (bi, 0, COL_QB + hp)),
            pl.BlockSpec((None, s, LANES), lambda bi, hp: (bi, 0, COL_KB + hp)),
            pl.BlockSpec((None, s, LANES), lambda bi, hp: (bi, 0, COL_VB + hp)),
        ],
        out_specs=pl.BlockSpec((None, s, LANES), lambda bi, hp: (bi, 0, hp)),
        out_shape=jax.ShapeDtypeStruct((b, s, WIDTH_B), F32),
        scratch_shapes=[pltpu.VMEM((ncfg, s, LANES), F32), pltpu.VMEM((ncfg, s, LANES), F32)],
        compiler_params=_cparams(("arbitrary", "arbitrary")),
        name="attn_b",
    )(p3, p3, p3)


def _samp_a_kernel(q_ref, kt_ref, vt_ref, kn_ref, vn_ref, sink_ref, o_ref):
    nt = (((1,), (1,)), ((), ()))
    for kv in range(N_KV_A):
        q = q_ref[0, kv * G_A:(kv + 1) * G_A, :]
        kt = kt_ref[0, kv]
        vt = vt_ref[0, kv]
        kn = kn_ref[0, kv:kv + 1, :]
        vn = vn_ref[0, kv:kv + 1, :]
        sk = sink_ref[kv]
        s = jnp.dot(q.astype(BF16), kt.astype(BF16), preferred_element_type=F32) * SCALE
        sn = jnp.sum(q * kn, axis=-1, keepdims=True) * SCALE
        m = jnp.maximum(jnp.maximum(jnp.max(s, axis=-1, keepdims=True), sn), sk)
        p = jnp.exp(s - m)
        pn = jnp.exp(sn - m)
        den = jnp.sum(p, axis=-1, keepdims=True) + pn + jnp.exp(sk - m)
        o = lax.dot_general(p.astype(BF16), vt.astype(BF16), nt, preferred_element_type=F32)
        o_ref[0, kv * G_A:(kv + 1) * G_A, :] = (o + pn * vn) / den


def _samp_a(qa_s, kt_a, vt_a, kn, vn, sink3):
    db = qa_s.shape[0]
    lb = kt_a.shape[-1]
    assert lb <= WINDOW_A
    return pl.pallas_call(
        _samp_a_kernel,
        grid=(db,),
        in_specs=[
            pl.BlockSpec((1, N_HEADS_A, HEAD_DIM), lambda b: (b, 0, 0)),
            pl.BlockSpec((1, N_KV_A, HEAD_DIM, lb), lambda b: (b, 0, 0, 0)),
            pl.BlockSpec((1, N_KV_A, HEAD_DIM, lb), lambda b: (b, 0, 0, 0)),
            pl.BlockSpec((1, N_KV_A, HEAD_DIM), lambda b: (b, 0, 0)),
            pl.BlockSpec((1, N_KV_A, HEAD_DIM), lambda b: (b, 0, 0)),
            pl.BlockSpec((N_KV_A, G_A, 1), lambda b: (0, 0, 0)),
        ],
        out_specs=pl.BlockSpec((1, N_HEADS_A, HEAD_DIM), lambda b: (b, 0, 0)),
        out_shape=jax.ShapeDtypeStruct((db, N_HEADS_A, HEAD_DIM), F32),
        compiler_params=_cparams(("arbitrary",)),
        name="samp_a",
    )(qa_s, kt_a, vt_a, kn, vn, sink3)


SAMP_B_HG = 8


def _samp_b_kernel(kt_ref, vt_ref, qt_ref, knt_ref, vnt_ref, o_ref):
    b = pl.program_id(1)
    lb = kt_ref.shape[-1]
    laneb = lax.broadcasted_iota(jnp.int32, (1, LANES), 1) == b

    def column(ref):
        return jnp.sum(jnp.where(laneb, ref[...], 0.0), axis=-1, keepdims=True)

    qcol = column(qt_ref)
    kncol = column(knt_ref)
    vncol = column(vnt_ref)
    hrow = lax.broadcasted_iota(jnp.int32, (SAMP_B_HG, 1), 0)
    s = jnp.zeros((SAMP_B_HG, lb), F32)
    sn = jnp.zeros((SAMP_B_HG, 1), F32)
    for h in range(SAMP_B_HG):
        hs = slice(h * HEAD_DIM, (h + 1) * HEAD_DIM)
        s_h = jnp.sum(kt_ref[0, h] * qcol[hs], axis=0, keepdims=True)
        sn_h = jnp.sum(qcol[hs] * kncol[hs], axis=0, keepdims=True)
        s = jnp.where(hrow == h, s_h, s)
        sn = jnp.where(hrow == h, sn_h, sn)
    s = s * SCALE
    sn = sn * SCALE
    dist = lb - lax.broadcasted_iota(jnp.int32, (1, lb), 1)
    ps, pns, lses = [], [], []
    for w, d in DILATED:
        valid = ((dist % d) == 0) & (dist <= w)
        m = jnp.maximum(jnp.max(jnp.where(valid, s, NEG_INF), axis=-1, keepdims=True), sn)
        p = jnp.where(valid, jnp.exp(s - m), 0.0)
        pn = jnp.exp(sn - m)
        den = jnp.sum(p, axis=-1, keepdims=True) + pn
        ps.append(p / den)
        pns.append(pn / den)
        lses.append(jnp.log(den) + m)
    mx = functools.reduce(jnp.maximum, lses)
    ws = [jnp.exp(l - mx) for l in lses]
    wsum = functools.reduce(lambda a, c: a + c, ws)
    pmix = functools.reduce(lambda a, c: a + c, [w_ * p_ for w_, p_ in zip(ws, ps)]) / wsum
    pnmix = functools.reduce(lambda a, c: a + c, [w_ * p_ for w_, p_ in zip(ws, pns)]) / wsum
    cols = []
    for h in range(SAMP_B_HG):
        hs = slice(h * HEAD_DIM, (h + 1) * HEAD_DIM)
        oc = jnp.sum(vt_ref[0, h] * pmix[h:h + 1, :], axis=-1, keepdims=True)
        cols.append(oc + pnmix[h:h + 1, :] * vncol[hs])
    ocol = jnp.concatenate(cols, axis=0)

    @pl.when(b == 0)
    def _():
        o_ref[...] = jnp.zeros_like(o_ref)

    o_ref[...] = jnp.where(laneb, ocol, o_ref[...])


def _samp_b(kt_b, vt_b, qt, knt, vnt):
    db, nh, hd, lb = kt_b.shape
    assert db == LANES
    rows = SAMP_B_HG * HEAD_DIM
    grid = (nh // SAMP_B_HG, db)
    return pl.pallas_call(
        _samp_b_kernel,
        grid=grid,
        in_specs=[
            pl.BlockSpec((1, SAMP_B_HG, hd, lb), lambda g, b: (b, g, 0, 0)),
            pl.BlockSpec((1, SAMP_B_HG, hd, lb), lambda g, b: (b, g, 0, 0)),
            pl.BlockSpec((rows, db), lambda g, b: (g, 0)),
            pl.BlockSpec((rows, db), lambda g, b: (g, 0)),
            pl.BlockSpec((rows, db), lambda g, b: (g, 0)),
        ],
        out_specs=pl.BlockSpec((rows, db), lambda g, b: (g, 0)),
        out_shape=jax.ShapeDtypeStruct((nh * hd, db), F32),
        compiler_params=_cparams(("arbitrary", "arbitrary")),
        name="samp_b",
    )(kt_b, vt_b, qt, knt, vnt)


MERGE_TN = 1024


def _merge_kernel(oa_ref, ob_ref, x_ref, ga_ref, gb_ref, w_ref, gf_ref, ht_ref, xt_ref, cat_ref, h_ref):
    j = pl.program_id(1)
    nj = pl.num_programs(1)

    @pl.when(j == 0)
    def _():
        oa = oa_ref[...]
        ob = ob_ref[...]
        ya = (oa * lax.rsqrt(jnp.mean(oa * oa, axis=-1, keepdims=True) + EPS)) * ga_ref[...]
        yb = (ob * lax.rsqrt(jnp.mean(ob * ob, axis=-1, keepdims=True) + EPS)) * gb_ref[...]
        cat_ref[:, :WIDTH_A] = ya.astype(BF16)
        cat_ref[:, WIDTH_A:] = yb.astype(BF16)

    h_ref[j] = x_ref[...] + jnp.dot(cat_ref[...], w_ref[...], preferred_element_type=F32)

    @pl.when(j == nj - 1)
    def _():
        n_chunks = D_MODEL // MERGE_TN
        ssq = None
        for c in range(n_chunks):
            hc = h_ref[c]
            part = jnp.sum(hc * hc, axis=-1, keepdims=True)
            ssq = part if ssq is None else ssq + part
        rinv = lax.rsqrt(ssq * (1.0 / D_MODEL) + EPS)
        for c in range(n_chunks):
            sl = slice(c * MERGE_TN, (c + 1) * MERGE_TN)
            hc = h_ref[c]
            ht_ref[sl, :] = hc.T
            xt_ref[sl, :] = ((hc * rinv) * gf_ref[:, sl]).T.astype(BF16)


def _merge(oa, ob, x2d, ga, gb, w_bf, gf, tm):
    t = x2d.shape[0]
    nj = D_MODEL // MERGE_TN
    grid = (t // tm, nj)
    return pl.pallas_call(
        _merge_kernel,
        grid=grid,
        in_specs=[
            pl.BlockSpec((tm, WIDTH_A), lambda i, j: (i, 0)),
            pl.BlockSpec((tm, WIDTH_B), lambda i, j: (i, 0)),
            pl.BlockSpec((tm, MERGE_TN), lambda i, j: (i, j)),
            pl.BlockSpec((1, WIDTH_A), lambda i, j: (0, 0)),
            pl.BlockSpec((1, WIDTH_B), lambda i, j: (0, 0)),
            pl.BlockSpec((D_MODEL, MERGE_TN), lambda i, j: (0, j)),
            pl.BlockSpec((1, D_MODEL), lambda i, j: (0, 0)),
        ],
        out_specs=[
            pl.BlockSpec((D_MODEL, tm), lambda i, j: (0, i)),
            pl.BlockSpec((D_MODEL, tm), lambda i, j: (0, i)),
        ],
        out_shape=[
            jax.ShapeDtypeStruct((D_MODEL, t), F32),
            jax.ShapeDtypeStruct((D_MODEL, t), BF16),
        ],
        scratch_shapes=[pltpu.VMEM((tm, D_MODEL), BF16), pltpu.VMEM((nj, tm, MERGE_TN), F32)],
        compiler_params=_cparams(("arbitrary", "arbitrary")),
        name="merge",
    )(oa, ob, x2d, ga, gb, w_bf, gf)


ROUTER_TM = 256


def _top_values(x, k):
    row = lax.broadcasted_iota(jnp.int32, (k, x.shape[1]), 0)
    vals = jnp.zeros((k, x.shape[1]), F32)
    cur = x
    for r in range(k):
        m = jnp.max(cur, axis=0, keepdims=True)
        vals = jnp.where(row == r, m, vals)
        if r + 1 < k:
            cur = jnp.where(cur == m, NEG_INF, cur)
    return vals


def _router_kernel(x_ref, wq_ref, k1_ref, k2_ref, th_ref, e1_ref, s2_ref, e2_ref):
    half = PEER_DKEY // 2
    qt = jnp.dot(wq_ref[...], x_ref[...], preferred_element_type=F32)
    s1_all = jnp.dot(k1_ref[0], qt[:half].astype(BF16), preferred_element_type=F32)
    s2_all = jnp.dot(k2_ref[0], qt[half:].astype(BF16), preferred_element_type=F32)
    for c in range(ROUTER_TM // LANES):
        sl = slice(c * LANES, (c + 1) * LANES)
        s1 = s1_all[:, sl]
        s2 = s2_all[:, sl]
        v1 = _top_values(s1, PEER_TOPK)
        v2 = _top_values(s2, PEER_TOPK)
        cands = [v1[0:1] + v2[0:8], v1[0:1] + v2[8:16]]
        cands += [v1[a:a + 1] + v2[0:8] for a in range(1, 8)]
        cands += [v1[8:16] + v2[0:1]]
        cand = jnp.concatenate(cands, axis=0)
        tk = _top_values(cand, PEER_TOPK)[PEER_TOPK - 1:PEER_TOPK]
        m1 = v1[0:1]
        m2 = v2[0:1]
        z = jnp.sum(jnp.where(cand >= tk, jnp.exp(cand - (m1 + m2)), 0.0), axis=0, keepdims=True)
        theta = jnp.full(s1.shape, jnp.inf, F32)
        for b in range(PEER_TOPK):
            vb = v2[b:b + 1]
            theta = jnp.where((s1 + vb) >= tk, vb, theta)
        theta = jnp.where(s1 >= v1[PEER_TOPK - 1:PEER_TOPK], theta, jnp.inf)
        th_ref[0, :, sl] = theta
        e1_ref[0, :, sl] = jnp.exp(s1 - m1) / z
        s2_ref[0, :, sl] = s2
        e2_ref[0, :, sl] = jnp.exp(s2 - m2)


def _router(xt, wqt, k1b, k2b):
    t = xt.shape[1]
    grid = (t // ROUTER_TM, PEER_HEADS)
    out = jax.ShapeDtypeStruct((PEER_HEADS, PEER_NKEYS, t), F32)
    ospec = pl.BlockSpec((1, PEER_NKEYS, ROUTER_TM), lambda i, h: (h, 0, i))
    return pl.pallas_call(
        _router_kernel,
        grid=grid,
        in_specs=[
            pl.BlockSpec((D_MODEL, ROUTER_TM), lambda i, h: (0, i)),
            pl.BlockSpec((PEER_DKEY, D_MODEL), lambda i, h: (h, 0)),
            pl.BlockSpec((1, PEER_NKEYS, PEER_DKEY // 2), lambda i, h: (h, 0, 0)),
            pl.BlockSpec((1, PEER_NKEYS, PEER_DKEY // 2), lambda i, h: (h, 0, 0)),
        ],
        out_specs=[ospec, ospec, ospec, ospec],
        out_shape=[out, out, out, out],
        compiler_params=_cparams(("arbitrary", "arbitrary")),
        name="router",
    )(xt, wqt, k1b, k2b)


EXP_TM = 512
EXP_EB = 512
EXP_IB = EXP_EB // PEER_NKEYS


def _experts_kernel(x_ref, u_ref, vt_ref, th_ref, e1_ref, s2_ref, e2_ref, o_ref, at_ref):
    e = pl.program_id(1)

    @pl.when(e == 0)
    def _():
        o_ref[...] = jnp.zeros_like(o_ref)

    ht = jnp.dot(u_ref[...], x_ref[...], preferred_element_type=F32)
    for il in range(EXP_IB):
        rs = slice(il * PEER_NKEYS, (il + 1) * PEER_NKEYS)
        for c in range(EXP_TM // LANES):
            sl = slice(c * LANES, (c + 1) * LANES)
            g = jnp.zeros((PEER_NKEYS, LANES), F32)
            for h in range(PEER_HEADS):
                th = th_ref[h, 0, il:il + 1, sl]
                e1 = e1_ref[h, 0, il:il + 1, sl]
                g = g + jnp.where(s2_ref[h, :, sl] >= th, e2_ref[h, :, sl], 0.0) * e1
            at_ref[rs, sl] = (jax.nn.gelu(ht[rs, sl]) * g).astype(BF16)
    o_ref[...] += jnp.dot(vt_ref[...], at_ref[...], preferred_element_type=F32)


def _experts(xt, u_bf, vt_bf, th, e1, s2, e2):
    t = xt.shape[1]
    grid = (t // EXP_TM, PEER_EXPERTS // EXP_EB)
    nblk = PEER_NKEYS // EXP_IB
    th4 = th.reshape(PEER_HEADS, nblk, EXP_IB, t)
    e14 = e1.reshape(PEER_HEADS, nblk, EXP_IB, t)
    return pl.pallas_call(
        _experts_kernel,
        grid=grid,
        in_specs=[
            pl.BlockSpec((D_MODEL, EXP_TM), lambda i, e: (0, i)),
            pl.BlockSpec((EXP_EB, D_MODEL), lambda i, e: (e, 0)),
            pl.BlockSpec((D_MODEL, EXP_EB), lambda i, e: (0, e)),
            pl.BlockSpec((PEER_HEADS, 1, EXP_IB, EXP_TM), lambda i, e: (0, e, 0, i)),
            pl.BlockSpec((PEER_HEADS, 1, EXP_IB, EXP_TM), lambda i, e: (0, e, 0, i)),
            pl.BlockSpec((PEER_HEADS, PEER_NKEYS, EXP_TM), lambda i, e: (0, 0, i)),
            pl.BlockSpec((PEER_HEADS, PEER_NKEYS, EXP_TM), lambda i, e: (0, 0, i)),
        ],
        out_specs=pl.BlockSpec((D_MODEL, EXP_TM), lambda i, e: (0, i)),
        out_shape=jax.ShapeDtypeStruct((D_MODEL, t), F32),
        scratch_shapes=[pltpu.VMEM((EXP_EB, EXP_TM), BF16)],
        compiler_params=_cparams(("arbitrary", "arbitrary")),
        name="experts",
    )(xt, u_bf, vt_bf, th4, e14, s2, e2)


def _final_kernel(ht_ref, ft_ref, g_ref, y_ref):
    h = ht_ref[...] + ft_ref[...]
    ms = jnp.mean(h * h, axis=0, keepdims=True)
    y = (h * lax.rsqrt(ms + EPS)) * g_ref[...]
    y_ref[...] = y.T


def _final(ht, ft, ft_block0, g_col, tm):
    t = ht.shape[1]
    return pl.pallas_call(
        _final_kernel,
        grid=(t // tm,),
        in_specs=[
            pl.BlockSpec((D_MODEL, tm), lambda i: (0, i)),
            pl.BlockSpec((D_MODEL, tm), lambda i: (0, ft_block0 + i)),
            pl.BlockSpec((D_MODEL, 1), lambda i: (0, 0)),
        ],
        out_specs=pl.BlockSpec((tm, D_MODEL), lambda i: (i, 0)),
        out_shape=jax.ShapeDtypeStruct((t, D_MODEL), F32),
        compiler_params=_cparams(("arbitrary",)),
        name="final",
    )(ht, ft, g_col)


def kernel(x_prompt, x_sample, state_a_k, state_a_v, state_b_k, state_b_v, g_attn, w_in, attn_sink,
           g_out_a, g_out_b, w_out, g_ffn, peer_wq, peer_k1, peer_k2, peer_u, peer_v, g_final):
    bsz, seq, _ = x_prompt.shape
    db, dt, _ = x_sample.shape
    assert w_in.shape[0] == 1 and dt == 1
    past = PAST_LEN
    lb = state_b_k.shape[2]
    assert lb == max(w for w, _ in DILATED)

    w_in_b = w_in[0].astype(BF16)
    w_out_b = w_out[0].astype(BF16)
    wq_t = peer_wq[0].reshape(D_MODEL, PEER_HEADS * PEER_DKEY).T.astype(BF16)
    k1_b = peer_k1[0].astype(BF16)
    k2_b = peer_k2[0].astype(BF16)
    u_b = peer_u[0].astype(BF16)
    vt_b = peer_v[0].T.astype(BF16)

    rope_flags = jnp.concatenate([
        jnp.ones((1, WIDTH_A + N_KV_A * HEAD_DIM), F32), jnp.zeros((1, N_KV_A * HEAD_DIM), F32),
        jnp.ones((1, 2 * WIDTH_B), F32), jnp.zeros((1, WIDTH_B), F32)], axis=-1)
    cos_p, sin_p = _rope_tables(jnp.arange(seq))
    cos_s, sin_s = _rope_tables(jnp.full((db,), past))

    tm_p = 512
    xp = x_prompt.reshape(bsz * seq, D_MODEL)
    pp = _proj(xp, g_attn, w_in_b, cos_p, sin_p, rope_flags, tm_p, seq // tm_p)
    pp3 = pp.reshape(bsz, seq, QKV_COLS)
    oa_p = _attn_a(pp3, attn_sink[0])
    ob_p = _attn_b(pp3)
    ht_p, xt_p = _merge(oa_p.reshape(bsz * seq, WIDTH_A), ob_p.reshape(bsz * seq, WIDTH_B), xp,
                        g_out_a, g_out_b, w_out_b, g_ffn, 256)

    xs = x_sample.reshape(db, D_MODEL)
    ps = _proj(xs, g_attn, w_in_b, cos_s, sin_s, rope_flags, db, 1)
    c_ka, c_va, c_qb, c_kb, c_vb = (COL_KA * LANES, COL_VA * LANES, COL_QB * LANES,
                                    COL_KB * LANES, COL_VB * LANES)
    qa_s = ps[:, :c_ka].reshape(db, N_HEADS_A, HEAD_DIM)
    ka_s = ps[:, c_ka:c_va].reshape(db, N_KV_A, HEAD_DIM)
    va_s = ps[:, c_va:c_qb].reshape(db, N_KV_A, HEAD_DIM)
    kb_s = ps[:, c_kb:c_vb]
    vb_s = ps[:, c_vb:]
    kt_a = jnp.transpose(state_a_k[0], (0, 2, 3, 1))
    vt_a = jnp.transpose(state_a_v[0], (0, 2, 3, 1))
    skt_b = jnp.transpose(state_b_k[0], (0, 2, 3, 1))
    svt_b = jnp.transpose(state_b_v[0], (0, 2, 3, 1))
    oa_s = _samp_a(qa_s, kt_a, vt_a, ka_s, va_s, attn_sink[0].reshape(N_KV_A, G_A, 1))
    obt_s = _samp_b(skt_b, svt_b, ps[:, c_qb:c_kb].T, kb_s.T, vb_s.T)
    ht_s, xt_s = _merge(oa_s.reshape(db, WIDTH_A), obt_s.T, xs, g_out_a, g_out_b, w_out_b, g_ffn, db)

    t_all = bsz * seq + db
    t_pad = -(-t_all // EXP_TM) * EXP_TM
    xt = jnp.concatenate([xt_p, xt_s, jnp.zeros((D_MODEL, t_pad - t_all), BF16)], axis=1)
    th, e1, s2, e2 = _router(xt, wq_t, k1_b, k2_b)
    ft = _experts(xt, u_b, vt_b, th, e1, s2, e2)

    g_col = g_final.reshape(D_MODEL, 1)
    y_p = _final(ht_p, ft, 0, g_col, 256).reshape(bsz, seq, D_MODEL)
    y_s = _final(ht_s, ft, (bsz * seq) // db, g_col, db).reshape(db, 1, D_MODEL)

    rows_a = min(WINDOW_A, seq)
    ka_p = pp3[:, seq - rows_a:, c_ka:c_va].reshape(1, bsz, rows_a, N_KV_A, HEAD_DIM)
    va_p = pp3[:, seq - rows_a:, c_va:c_qb].reshape(1, bsz, rows_a, N_KV_A, HEAD_DIM)
    kb_p = pp3[:, :, c_kb:c_vb].reshape(1, bsz, seq, N_HEADS_B, HEAD_DIM)
    vb_p = pp3[:, :, c_vb:].reshape(1, bsz, seq, N_HEADS_B, HEAD_DIM)
    return (y_p, y_s, ka_p, va_p, kb_p, vb_p,
            ka_s.reshape(1, db, 1, N_KV_A, HEAD_DIM), va_s.reshape(1, db, 1, N_KV_A, HEAD_DIM),
            kb_s.reshape(1, db, 1, N_HEADS_B, HEAD_DIM), vb_s.reshape(1, db, 1, N_HEADS_B, HEAD_DIM))
```

```python
import functools
import math

import jax
import jax.numpy as jnp
import numpy as np
from jax import lax
from jax.experimental import pallas as pl
from jax.experimental.pallas import tpu as pltpu

D_MODEL = 4096
HEAD_DIM = 64
WIDTH_A = 2048
WIDTH_B = 2048
N_HEADS_A = 32
N_KV_A = 4
G_A = 8
N_HEADS_B = 32
WINDOW_A = 128
DILATED = ((128, 1), (512, 4), (2048, 16))
BLOCK = 128
PAST_LEN = 8192
ROPE_THETA = 10000.0
SCALE = HEAD_DIM ** -0.5
PEER_HEADS = 8
PEER_NKEYS = 128
PEER_EXPERTS = PEER_NKEYS * PEER_NKEYS
PEER_DKEY = 256
PEER_TOPK = 16
EPS = 1e-6
QKV_COLS = WIDTH_A + 2 * N_KV_A * HEAD_DIM + 3 * WIDTH_B

COL_QA = 0
COL_KA = WIDTH_A // 128
COL_VA = COL_KA + N_KV_A * HEAD_DIM // 128
COL_QB = COL_VA + N_KV_A * HEAD_DIM // 128
COL_KB = COL_QB + WIDTH_B // 128
COL_VB = COL_KB + WIDTH_B // 128

LANES = 128
VMEM_LIMIT = 60 * 1024 * 1024

BF16 = jnp.bfloat16
F32 = jnp.float32
NEG_INF = float("-inf")


def _cparams(sem):
    return pltpu.CompilerParams(dimension_semantics=sem, vmem_limit_bytes=VMEM_LIMIT)


PROJ_TN = 512
PROJ_FIRST_PLAIN_BLOCK = COL_VB * LANES // PROJ_TN
assert COL_VB * LANES % PROJ_TN == 0


def _proj_kernel(x_ref, g_ref, w_ref, cos_ref, sin_ref, rope_ref, o_ref, xn_ref):
    j = pl.program_id(1)

    @pl.when(j == 0)
    def _():
        x = x_ref[...]
        ms = jnp.mean(x * x, axis=-1, keepdims=True)
        xn_ref[...] = ((x * lax.rsqrt(ms + EPS)) * g_ref[...]).astype(BF16)

    p = jnp.dot(xn_ref[...], w_ref[...], preferred_element_type=F32)

    @pl.when(j >= PROJ_FIRST_PLAIN_BLOCK)
    def _():
        o_ref[...] = p

    @pl.when(j < PROJ_FIRST_PLAIN_BLOCK)
    def _():
        lane = lax.broadcasted_iota(jnp.int32, (1, LANES), 1)
        first_half = (lane % HEAD_DIM) < (HEAD_DIM // 2)
        cos = cos_ref[...]
        sin = sin_ref[...]
        for c in range(PROJ_TN // LANES):
            sl = slice(c * LANES, (c + 1) * LANES)
            pc = p[:, sl]
            partner = jnp.where(first_half, pltpu.roll(pc, LANES - HEAD_DIM // 2, 1),
                                pltpu.roll(pc, HEAD_DIM // 2, 1))
            roped = pc * cos + partner * sin
            o_ref[:, sl] = jnp.where(rope_ref[:, sl] > 0.0, roped, pc)


def _proj(x2d, g, w_bf, cos_t, sin_t, rope_flags, tm, pos_blocks):
    t = x2d.shape[0]
    grid = (t // tm, QKV_COLS // PROJ_TN)
    return pl.pallas_call(
        _proj_kernel,
        grid=grid,
        in_specs=[
            pl.BlockSpec((tm, D_MODEL), lambda i, j: (i, 0)),
            pl.BlockSpec((1, D_MODEL), lambda i, j: (0, 0)),
            pl.BlockSpec((D_MODEL, PROJ_TN), lambda i, j: (0, j)),
            pl.BlockSpec((tm, LANES), lambda i, j: (i % pos_blocks, 0)),
            pl.BlockSpec((tm, LANES), lambda i, j: (i % pos_blocks, 0)),
            pl.BlockSpec((1, PROJ_TN), lambda i, j: (0, j)),
        ],
        out_specs=pl.BlockSpec((tm, PROJ_TN), lambda i, j: (i, j)),
        out_shape=jax.ShapeDtypeStruct((t, QKV_COLS), F32),
        scratch_shapes=[pltpu.VMEM((tm, D_MODEL), BF16)],
        compiler_params=_cparams(("arbitrary", "arbitrary")),
        name="proj",
    )(x2d, g, w_bf, cos_t, sin_t, rope_flags)


def _rope_tables(pos):
    half = HEAD_DIM // 2
    inv = ROPE_THETA ** (-jnp.arange(half, dtype=F32) / half)
    ang = pos.astype(F32)[:, None] * inv[None, :]
    cos = jnp.cos(ang)
    sin = jnp.sin(ang)
    cos_t = jnp.concatenate([cos, cos, cos, cos], axis=-1)
    sin_t = jnp.concatenate([-sin, sin, -sin, sin], axis=-1)
    return cos_t, sin_t


def _attn_a_kernel(sink_ref, q_ref, kp_ref, kc_ref, vp_ref, vc_ref, o_ref):
    kp_id = pl.program_id(1)
    n = pl.program_id(2)
    lane = lax.broadcasted_iota(jnp.int32, (1, LANES), 1)
    lo = lane < HEAD_DIM
    qi = lax.broadcasted_iota(jnp.int32, (BLOCK, 2 * BLOCK), 0)
    kj = lax.broadcasted_iota(jnp.int32, (BLOCK, 2 * BLOCK), 1)
    first_key = jnp.where(n == 0, BLOCK, 0)
    valid = (kj >= qi) & (kj <= qi + WINDOW_A) & (kj >= first_key)
    k2 = jnp.concatenate([kp_ref[...], kc_ref[...]], axis=0)
    v2 = jnp.concatenate([vp_ref[...], vc_ref[...]], axis=0)
    nt = (((1,), (1,)), ((), ()))
    for kvl in range(2):
        if kvl == 0:
            k_lo = jnp.where(lo, k2, 0.0)
            k_hi = pltpu.roll(k_lo, HEAD_DIM, 1)
            v_lo = jnp.where(lo, v2, 0.0)
            v_hi = pltpu.roll(v_lo, HEAD_DIM, 1)
        else:
            k_hi = jnp.where(lo, 0.0, k2)
            k_lo = pltpu.roll(k_hi, HEAD_DIM, 1)
            v_hi = jnp.where(lo, 0.0, v2)
            v_lo = pltpu.roll(v_hi, HEAD_DIM, 1)
        kb = (k_lo.astype(BF16), k_hi.astype(BF16))
        vb = (v_lo.astype(BF16), v_hi.astype(BF16))
        scores = []
        for c in range(G_A // 2):
            col = (kvl * (G_A // 2) + c) * LANES
            qc = q_ref[:, col:col + LANES].astype(BF16)
            for par in range(2):
                scores.append(lax.dot_general(qc, kb[par], nt, preferred_element_type=F32) * SCALE)
        probs = []
        for i, s in enumerate(scores):
            sk = sink_ref[kp_id * (2 * G_A) + kvl * G_A + i]
            s = jnp.where(valid, s, NEG_INF)
            m = jnp.maximum(jnp.max(s, axis=-1, keepdims=True), sk)
            p = jnp.exp(s - m)
            den = jnp.sum(p, axis=-1, keepdims=True) + jnp.exp(sk - m)
            probs.append((p.astype(BF16), 1.0 / den))
        for c in range(G_A // 2):
            col = (kvl * (G_A // 2) + c) * LANES
            acc = jnp.zeros((BLOCK, LANES), F32)
            for par in range(2):
                p, rden = probs[2 * c + par]
                acc = acc + jnp.dot(p, vb[par], preferred_element_type=F32) * rden
            o_ref[:, col:col + LANES] = acc


def _attn_a(p3, sink):
    b, s, _ = p3.shape
    nb = s // BLOCK
    qw = 2 * G_A * HEAD_DIM
    grid = (b, N_KV_A // 2, nb)
    prev = lambda n: jnp.maximum(n - 1, 0)
    return pl.pallas_call(
        _attn_a_kernel,
        grid=grid,
        in_specs=[
            pl.BlockSpec(memory_space=pltpu.SMEM),
            pl.BlockSpec((None, BLOCK, qw), lambda bi, kp, n: (bi, n, kp)),
            pl.BlockSpec((None, BLOCK, LANES), lambda bi, kp, n: (bi, prev(n), COL_KA + kp)),
            pl.BlockSpec((None, BLOCK, LANES), lambda bi, kp, n: (bi, n, COL_KA + kp)),
            pl.BlockSpec((None, BLOCK, LANES), lambda bi, kp, n: (bi, prev(n), COL_VA + kp)),
            pl.BlockSpec((None, BLOCK, LANES), lambda bi, kp, n: (bi, n, COL_VA + kp)),
        ],
        out_specs=pl.BlockSpec((None, BLOCK, qw), lambda bi, kp, n: (bi, n, kp)),
        out_shape=jax.ShapeDtypeStruct((b, s, WIDTH_A), F32),
        compiler_params=_cparams(("arbitrary", "arbitrary", "arbitrary")),
        name="attn_a",
    )(sink, p3, p3, p3, p3, p3)


def _attn_b_kernel(q_ref, k_ref, v_ref, o_ref, oc_ref, lc_ref):
    seq = q_ref.shape[0]
    lane = lax.broadcasted_iota(jnp.int32, (1, LANES), 1)
    lo = lane < HEAD_DIM
    nt = (((1,), (1,)), ((), ()))

    masks = (lo, jnp.logical_not(lo))

    def blocks(cfg, d, starts, has_prev):
        nk = 2 * BLOCK if has_prev else BLOCK
        qi = lax.broadcasted_iota(jnp.int32, (BLOCK, nk), 0)
        kj = lax.broadcasted_iota(jnp.int32, (BLOCK, nk), 1)
        if has_prev:
            valid = (kj >= qi) & (kj <= qi + BLOCK)
        else:
            valid = kj <= qi
        rows_qs, vs, scores = [], [], []
        for start in starts:
            rows_q = pl.ds(start, BLOCK, stride=d) if d > 1 else pl.ds(start, BLOCK)
            kstart = start - BLOCK * d if has_prev else start
            rows_k = pl.ds(kstart, nk, stride=d) if d > 1 else pl.ds(kstart, nk)
            q = q_ref[rows_q, :].astype(BF16)
            k = k_ref[rows_k, :]
            rows_qs.append(rows_q)
            vs.append(v_ref[rows_k, :])
            for msk in masks:
                kpar = jnp.where(msk, k, 0.0).astype(BF16)
                scores.append(lax.dot_general(q, kpar, nt, preferred_element_type=F32) * SCALE)
        probs = []
        for s in scores:
            s = jnp.where(valid, s, NEG_INF)
            m = jnp.max(s, axis=-1, keepdims=True)
            p = jnp.exp(s - m)
            den = jnp.sum(p, axis=-1, keepdims=True)
            probs.append((p.astype(BF16), 1.0 / den, jnp.log(den) + m))
        for b, (rows_q, v) in enumerate(zip(rows_qs, vs)):
            acc = jnp.zeros((BLOCK, LANES), F32)
            lse = jnp.zeros((BLOCK, LANES), F32)
            for par, msk in enumerate(masks):
                p, rden, l = probs[2 * b + par]
                vpar = jnp.where(msk, v, 0.0).astype(BF16)
                acc = acc + jnp.dot(p, vpar, preferred_element_type=F32) * rden
                lse = jnp.where(msk, l, lse)
            oc_ref[cfg, rows_q, :] = acc
            lc_ref[cfg, rows_q, :] = lse

    def block(cfg, d, start, has_prev):
        blocks(cfg, d, [start], has_prev)

    for cfg, (w, d) in enumerate(DILATED):
        assert w // d == BLOCK
        nblk = seq // d // BLOCK
        span = BLOCK * d
        if d == 1:
            block(cfg, d, 0, False)
            cnt = nblk - 1
            group = next(gs for gs in (5, 4, 3, 2, 1) if cnt % gs == 0)

            def body_seq(it, carry, cfg=cfg, d=d, span=span, group=group):
                blocks(cfg, d, [(1 + it * group + u) * span for u in range(group)], True)
                return carry

            lax.fori_loop(0, cnt // group, body_seq, 0)
            continue
        cg = min(d, 4)
        ncg = d // cg

        def body_first(it, carry, cfg=cfg, d=d, cg=cg):
            blocks(cfg, d, [it * cg + u for u in range(cg)], False)
            return carry

        if ncg == 1:
            body_first(0, 0)
        else:
            lax.fori_loop(0, ncg, body_first, 0)
        if nblk > 1:
            def body_rest(it, carry, cfg=cfg, d=d, cg=cg, ncg=ncg, span=span):
                n = 1 + it // ncg
                rc = it % ncg
                blocks(cfg, d, [rc * cg + u + n * span for u in range(cg)], True)
                return carry

            lax.fori_loop(0, (nblk - 1) * ncg, body_rest, 0)

    def combine(i, carry):
        rows = pl.ds(pl.multiple_of(i * BLOCK, BLOCK), BLOCK)
        l0 = lc_ref[0, rows, :]
        l1 = lc_ref[1, rows, :]
        l2 = lc_ref[2, rows, :]
        mx = jnp.maximum(jnp.maximum(l0, l1), l2)
        w0 = jnp.exp(l0 - mx)
        w1 = jnp.exp(l1 - mx)
        w2 = jnp.exp(l2 - mx)
        num = w0 * oc_ref[0, rows, :] + w1 * oc_ref[1, rows, :] + w2 * oc_ref[2, rows, :]
        o_ref[rows, :] = num / (w0 + w1 + w2)
        return carry

    lax.fori_loop(0, seq // BLOCK, combine, 0)


def _attn_b(p3):
    b, s, _ = p3.shape
    grid = (b, N_HEADS_B // 2)
    ncfg = len(DILATED)
    return pl.pallas_call(
        _attn_b_kernel,
        grid=grid,
        in_specs=[
            pl.BlockSpec((None, s, LANES), lambda bi, hp: (bi, 0, COL_QB + hp)),
            pl.BlockSpec((None, s, LANES), lambda bi, hp: (bi, 0, COL_KB + hp)),
            pl.BlockSpec((None, s, LANES), lambda bi, hp: (bi, 0, COL_VB + hp)),
        ],
        out_specs=pl.BlockSpec((None, s, LANES), lambda bi, hp: (bi, 0, hp)),
        out_shape=jax.ShapeDtypeStruct((b, s, WIDTH_B), F32),
        scratch_shapes=[pltpu.VMEM((ncfg, s, LANES), F32), pltpu.VMEM((ncfg, s, LANES), F32)],
        compiler_params=_cparams(("arbitrary", "arbitrary")),
        name="attn_b",
    )(p3, p3, p3)


def _samp_a_kernel(q_ref, kt_ref, vt_ref, kn_ref, vn_ref, sink_ref, o_ref):
    nt = (((1,), (1,)), ((), ()))
    for kv in range(N_KV_A):
        q = q_ref[0, kv * G_A:(kv + 1) * G_A, :]
        kt = kt_ref[0, kv]
        vt = vt_ref[0, kv]
        kn = kn_ref[0, kv:kv + 1, :]
        vn = vn_ref[0, kv:kv + 1, :]
        sk = sink_ref[kv]
        s = jnp.dot(q.astype(BF16), kt.astype(BF16), preferred_element_type=F32) * SCALE
        sn = jnp.sum(q * kn, axis=-1, keepdims=True) * SCALE
        m = jnp.maximum(jnp.maximum(jnp.max(s, axis=-1, keepdims=True), sn), sk)
        p = jnp.exp(s - m)
        pn = jnp.exp(sn - m)
        den = jnp.sum(p, axis=-1, keepdims=True) + pn + jnp.exp(sk - m)
        o = lax.dot_general(p.astype(BF16), vt.astype(BF16), nt, preferred_element_type=F32)
        o_ref[0, kv * G_A:(kv + 1) * G_A, :] = (o + pn * vn) / den


def _samp_a(qa_s, kt_a, vt_a, kn, vn, sink3):
    db = qa_s.shape[0]
    lb = kt_a.shape[-1]
    assert lb <= WINDOW_A
    return pl.pallas_call(
        _samp_a_kernel,
        grid=(db,),
        in_specs=[
            pl.BlockSpec((1, N_HEADS_A, HEAD_DIM), lambda b: (b, 0, 0)),
            pl.BlockSpec((1, N_KV_A, HEAD_DIM, lb), lambda b: (b, 0, 0, 0)),
            pl.BlockSpec((1, N_KV_A, HEAD_DIM, lb), lambda b: (b, 0, 0, 0)),
            pl.BlockSpec((1, N_KV_A, HEAD_DIM), lambda b: (b, 0, 0)),
            pl.BlockSpec((1, N_KV_A, HEAD_DIM), lambda b: (b, 0, 0)),
            pl.BlockSpec((N_KV_A, G_A, 1), lambda b: (0, 0, 0)),
        ],
        out_specs=pl.BlockSpec((1, N_HEADS_A, HEAD_DIM), lambda b: (b, 0, 0)),
        out_shape=jax.ShapeDtypeStruct((db, N_HEADS_A, HEAD_DIM), F32),
        compiler_params=_cparams(("arbitrary",)),
        name="samp_a",
    )(qa_s, kt_a, vt_a, kn, vn, sink3)


SAMP_B_HG = 8


def _samp_b_kernel(kt_ref, vt_ref, qt_ref, knt_ref, vnt_ref, o_ref):
    b = pl.program_id(1)
    lb = kt_ref.shape[-1]
    laneb = lax.broadcasted_iota(jnp.int32, (1, LANES), 1) == b

    def column(ref):
        return jnp.sum(jnp.where(laneb, ref[...], 0.0), axis=-1, keepdims=True)

    qcol = column(qt_ref)
    kncol = column(knt_ref)
    vncol = column(vnt_ref)
    hrow = lax.broadcasted_iota(jnp.int32, (SAMP_B_HG, 1), 0)
    s = jnp.zeros((SAMP_B_HG, lb), F32)
    sn = jnp.zeros((SAMP_B_HG, 1), F32)
    for h in range(SAMP_B_HG):
        hs = slice(h * HEAD_DIM, (h + 1) * HEAD_DIM)
        s_h = jnp.sum(kt_ref[0, h] * qcol[hs], axis=0, keepdims=True)
        sn_h = jnp.sum(qcol[hs] * kncol[hs], axis=0, keepdims=True)
        s = jnp.where(hrow == h, s_h, s)
        sn = jnp.where(hrow == h, sn_h, sn)
    s = s * SCALE
    sn = sn * SCALE
    dist = lb - lax.broadcasted_iota(jnp.int32, (1, lb), 1)
    ps, pns, lses = [], [], []
    for w, d in DILATED:
        valid = ((dist % d) == 0) & (dist <= w)
        m = jnp.maximum(jnp.max(jnp.where(valid, s, NEG_INF), axis=-1, keepdims=True), sn)
        p = jnp.where(valid, jnp.exp(s - m), 0.0)
        pn = jnp.exp(sn - m)
        den = jnp.sum(p, axis=-1, keepdims=True) + pn
        ps.append(p / den)
        pns.append(pn / den)
        lses.append(jnp.log(den) + m)
    mx = functools.reduce(jnp.maximum, lses)
    ws = [jnp.exp(l - mx) for l in lses]
    wsum = functools.reduce(lambda a, c: a + c, ws)
    pmix = functools.reduce(lambda a, c: a + c, [w_ * p_ for w_, p_ in zip(ws, ps)]) / wsum
    pnmix = functools.reduce(lambda a, c: a + c, [w_ * p_ for w_, p_ in zip(ws, pns)]) / wsum
    cols = []
    for h in range(SAMP_B_HG):
        hs = slice(h * HEAD_DIM, (h + 1) * HEAD_DIM)
        oc = jnp.sum(vt_ref[0, h] * pmix[h:h + 1, :], axis=-1, keepdims=True)
        cols.append(oc + pnmix[h:h + 1, :] * vncol[hs])
    ocol = jnp.concatenate(cols, axis=0)

    @pl.when(b == 0)
    def _():
        o_ref[...] = jnp.zeros_like(o_ref)

    o_ref[...] = jnp.where(laneb, ocol, o_ref[...])


def _samp_b(kt_b, vt_b, qt, knt, vnt):
    db, nh, hd, lb = kt_b.shape
    assert db == LANES
    rows = SAMP_B_HG * HEAD_DIM
    grid = (nh // SAMP_B_HG, db)
    return pl.pallas_call(
        _samp_b_kernel,
        grid=grid,
        in_specs=[
            pl.BlockSpec((1, SAMP_B_HG, hd, lb), lambda g, b: (b, g, 0, 0)),
            pl.BlockSpec((1, SAMP_B_HG, hd, lb), lambda g, b: (b, g, 0, 0)),
            pl.BlockSpec((rows, db), lambda g, b: (g, 0)),
            pl.BlockSpec((rows, db), lambda g, b: (g, 0)),
            pl.BlockSpec((rows, db), lambda g, b: (g, 0)),
        ],
        out_specs=pl.BlockSpec((rows, db), lambda g, b: (g, 0)),
        out_shape=jax.ShapeDtypeStruct((nh * hd, db), F32),
        compiler_params=_cparams(("arbitrary", "arbitrary")),
        name="samp_b",
    )(kt_b, vt_b, qt, knt, vnt)


MERGE_TN = 1024


def _merge_kernel(oa_ref, ob_ref, x_ref, ga_ref, gb_ref, w_ref, gf_ref, ht_ref, xt_ref, cat_ref, h_ref):
    j = pl.program_id(1)
    nj = pl.num_programs(1)

    @pl.when(j == 0)
    def _():
        oa = oa_ref[...]
        ob = ob_ref[...]
        ya = (oa * lax.rsqrt(jnp.mean(oa * oa, axis=-1, keepdims=True) + EPS)) * ga_ref[...]
        yb = (ob * lax.rsqrt(jnp.mean(ob * ob, axis=-1, keepdims=True) + EPS)) * gb_ref[...]
        cat_ref[:, :WIDTH_A] = ya.astype(BF16)
        cat_ref[:, WIDTH_A:] = yb.astype(BF16)

    h_ref[j] = x_ref[...] + jnp.dot(cat_ref[...], w_ref[...], preferred_element_type=F32)

    @pl.when(j == nj - 1)
    def _():
        n_chunks = D_MODEL // MERGE_TN
        ssq = None
        for c in range(n_chunks):
            hc = h_ref[c]
            part = jnp.sum(hc * hc, axis=-1, keepdims=True)
            ssq = part if ssq is None else ssq + part
        rinv = lax.rsqrt(ssq * (1.0 / D_MODEL) + EPS)
        for c in range(n_chunks):
            sl = slice(c * MERGE_TN, (c + 1) * MERGE_TN)
            hc = h_ref[c]
            ht_ref[sl, :] = hc.T
            xt_ref[sl, :] = ((hc * rinv) * gf_ref[:, sl]).T.astype(BF16)


def _merge(oa, ob, x2d, ga, gb, w_bf, gf, tm):
    t = x2d.shape[0]
    nj = D_MODEL // MERGE_TN
    grid = (t // tm, nj)
    return pl.pallas_call(
        _merge_kernel,
        grid=grid,
        in_specs=[
            pl.BlockSpec((tm, WIDTH_A), lambda i, j: (i, 0)),
            pl.BlockSpec((tm, WIDTH_B), lambda i, j: (i, 0)),
            pl.BlockSpec((tm, MERGE_TN), lambda i, j: (i, j)),
            pl.BlockSpec((1, WIDTH_A), lambda i, j: (0, 0)),
            pl.BlockSpec((1, WIDTH_B), lambda i, j: (0, 0)),
            pl.BlockSpec((D_MODEL, MERGE_TN), lambda i, j: (0, j)),
            pl.BlockSpec((1, D_MODEL), lambda i, j: (0, 0)),
        ],
        out_specs=[
            pl.BlockSpec((D_MODEL, tm), lambda i, j: (0, i)),
            pl.BlockSpec((D_MODEL, tm), lambda i, j: (0, i)),
        ],
        out_shape=[
            jax.ShapeDtypeStruct((D_MODEL, t), F32),
            jax.ShapeDtypeStruct((D_MODEL, t), BF16),
        ],
        scratch_shapes=[pltpu.VMEM((tm, D_MODEL), BF16), pltpu.VMEM((nj, tm, MERGE_TN), F32)],
        compiler_params=_cparams(("arbitrary", "arbitrary")),
        name="merge",
    )(oa, ob, x2d, ga, gb, w_bf, gf)


ROUTER_TM = 512


def _top_values(x, k):
    row = lax.broadcasted_iota(jnp.int32, (k, x.shape[1]), 0)
    vals = jnp.zeros((k, x.shape[1]), F32)
    cur = x
    for r in range(k):
        m = jnp.max(cur, axis=0, keepdims=True)
        vals = jnp.where(row == r, m, vals)
        if r + 1 < k:
            cur = jnp.where(cur == m, NEG_INF, cur)
    return vals


def _router_kernel(x_ref, wq_ref, k1_ref, k2_ref, th_ref, e1_ref, s2_ref, e2_ref):
    half = PEER_DKEY // 2
    qt = jnp.dot(wq_ref[...], x_ref[...], preferred_element_type=F32)
    s1_all = jnp.dot(k1_ref[0], qt[:half].astype(BF16), preferred_element_type=F32)
    s2_all = jnp.dot(k2_ref[0], qt[half:].astype(BF16), preferred_element_type=F32)
    for c in range(ROUTER_TM // LANES):
        sl = slice(c * LANES, (c + 1) * LANES)
        s1 = s1_all[:, sl]
        s2 = s2_all[:, sl]
        v1 = _top_values(s1, PEER_TOPK)
        v2 = _top_values(s2, PEER_TOPK)
        cands = [v1[0:1] + v2[0:8], v1[0:1] + v2[8:16]]
        cands += [v1[a:a + 1] + v2[0:8] for a in range(1, 8)]
        cands += [v1[8:16] + v2[0:1]]
        cand = jnp.concatenate(cands, axis=0)
        tk = _top_values(cand, PEER_TOPK)[PEER_TOPK - 1:PEER_TOPK]
        m1 = v1[0:1]
        m2 = v2[0:1]
        z = jnp.sum(jnp.where(cand >= tk, jnp.exp(cand - (m1 + m2)), 0.0), axis=0, keepdims=True)
        theta = jnp.full(s1.shape, jnp.inf, F32)
        for b in range(PEER_TOPK):
            vb = v2[b:b + 1]
            theta = jnp.where((s1 + vb) >= tk, vb, theta)
        theta = jnp.where(s1 >= v1[PEER_TOPK - 1:PEER_TOPK], theta, jnp.inf)
        th_ref[0, :, sl] = theta
        e1_ref[0, :, sl] = jnp.exp(s1 - m1) / z
        s2_ref[0, :, sl] = s2
        e2_ref[0, :, sl] = jnp.exp(s2 - m2)


def _router(xt, wqt, k1b, k2b):
    t = xt.shape[1]
    grid = (t // ROUTER_TM, PEER_HEADS)
    out = jax.ShapeDtypeStruct((PEER_HEADS, PEER_NKEYS, t), F32)
    ospec = pl.BlockSpec((1, PEER_NKEYS, ROUTER_TM), lambda i, h: (h, 0, i))
    return pl.pallas_call(
        _router_kernel,
        grid=grid,
        in_specs=[
            pl.BlockSpec((D_MODEL, ROUTER_TM), lambda i, h: (0, i)),
            pl.BlockSpec((PEER_DKEY, D_MODEL), lambda i, h: (h, 0)),
            pl.BlockSpec((1, PEER_NKEYS, PEER_DKEY // 2), lambda i, h: (h, 0, 0)),
            pl.BlockSpec((1, PEER_NKEYS, PEER_DKEY // 2), lambda i, h: (h, 0, 0)),
        ],
        out_specs=[ospec, ospec, ospec, ospec],
        out_shape=[out, out, out, out],
        compiler_params=_cparams(("arbitrary", "arbitrary")),
        name="router",
    )(xt, wqt, k1b, k2b)


EXP_TM = 512
EXP_EB = 512
EXP_IB = EXP_EB // PEER_NKEYS
EXP_STEPS = PEER_EXPERTS // (2 * EXP_EB)
ACT_ROWS = 64
DOWN_CHUNKS = 8


def _expert_act_pieces(ht_ref, slot, half, th_ref, e1_ref, s2_ref, e2_ref, at_ref):
    def piece(il, c, jc):
        row = half * EXP_IB + il
        sl = slice(c * LANES, (c + 1) * LANES)
        js = slice(jc * ACT_ROWS, (jc + 1) * ACT_ROWS)
        rs = slice(il * PEER_NKEYS + jc * ACT_ROWS, il * PEER_NKEYS + (jc + 1) * ACT_ROWS)
        g = jnp.zeros((ACT_ROWS, LANES), F32)
        for h in range(PEER_HEADS):
            th = th_ref[h, 0, row:row + 1, sl]
            e1 = e1_ref[h, 0, row:row + 1, sl]
            g = g + jnp.where(s2_ref[h, js, sl] >= th, e2_ref[h, js, sl], 0.0) * e1
        at_ref[rs, sl] = (jax.nn.gelu(ht_ref[slot, rs, sl]) * g).astype(BF16)

    return [functools.partial(piece, il, c, jc)
            for il in range(EXP_IB) for c in range(EXP_TM // LANES) for jc in range(PEER_NKEYS // ACT_ROWS)]


def _experts_kernel(x_ref, u_ref, vtp_ref, vtc_ref, th_ref, e1_ref, s2_ref, e2_ref, o_ref,
                    ata_ref, atb_ref, ht_ref):
    g_id = pl.program_id(1)
    gates = (th_ref, e1_ref, s2_ref, e2_ref)
    dyn0 = jnp.minimum(g_id, 0)

    def up(half):
        ht_ref[half] = jnp.dot(u_ref[half * EXP_EB:(half + 1) * EXP_EB, :], x_ref[...],
                               preferred_element_type=F32)

    def acts(half, at_ref):
        return _expert_act_pieces(ht_ref, dyn0 + half, half, *gates, at_ref)

    def down(vt_ref, at_ref, pieces, assign=False):
        per = -(-len(pieces) // DOWN_CHUNKS)
        rows = D_MODEL // DOWN_CHUNKS
        for k in range(DOWN_CHUNKS):
            rs = slice(k * rows, (k + 1) * rows)
            d = jnp.dot(vt_ref[rs, :], at_ref[...], preferred_element_type=F32)
            if assign:
                o_ref[rs, :] = d
            else:
                o_ref[rs, :] += d
            for p in pieces[k * per:(k + 1) * per]:
                p()

    @pl.when(g_id == 0)
    def _():
        up(0)
        up(1)
        for p in acts(0, ata_ref):
            p()
        down(vtc_ref, ata_ref, acts(1, atb_ref), assign=True)

    @pl.when((g_id > 0) & (g_id < EXP_STEPS))
    def _():
        up(0)
        down(vtp_ref, atb_ref, acts(0, ata_ref))
        up(1)
        down(vtc_ref, ata_ref, acts(1, atb_ref))

    @pl.when(g_id == EXP_STEPS)
    def _():
        down(vtp_ref, atb_ref, [])


def _experts(xt, u_bf, vt_bf, th, e1, s2, e2):
    t = xt.shape[1]
    grid = (t // EXP_TM, EXP_STEPS + 1)
    th4 = th.reshape(PEER_HEADS, EXP_STEPS, 2 * EXP_IB, t)
    e14 = e1.reshape(PEER_HEADS, EXP_STEPS, 2 * EXP_IB, t)
    last = EXP_STEPS - 1
    once = pl.Buffered(1)
    return pl.pallas_call(
        _experts_kernel,
        grid=grid,
        in_specs=[
            pl.BlockSpec((D_MODEL, EXP_TM), lambda i, g: (0, i), pipeline_mode=once),
            pl.BlockSpec((2 * EXP_EB, D_MODEL), lambda i, g: (jnp.minimum(g, last), 0)),
            pl.BlockSpec((D_MODEL, EXP_EB), lambda i, g: (0, jnp.maximum(2 * g - 1, 0))),
            pl.BlockSpec((D_MODEL, EXP_EB), lambda i, g: (0, jnp.minimum(2 * g, 2 * last))),
            pl.BlockSpec((PEER_HEADS, 1, 2 * EXP_IB, EXP_TM), lambda i, g: (0, jnp.minimum(g, last), 0, i)),
            pl.BlockSpec((PEER_HEADS, 1, 2 * EXP_IB, EXP_TM), lambda i, g: (0, jnp.minimum(g, last), 0, i)),
            pl.BlockSpec((PEER_HEADS, PEER_NKEYS, EXP_TM), lambda i, g: (0, 0, i), pipeline_mode=once),
            pl.BlockSpec((PEER_HEADS, PEER_NKEYS, EXP_TM), lambda i, g: (0, 0, i), pipeline_mode=once),
        ],
        out_specs=pl.BlockSpec((D_MODEL, EXP_TM), lambda i, g: (0, i), pipeline_mode=once),
        out_shape=jax.ShapeDtypeStruct((D_MODEL, t), F32),
        scratch_shapes=[pltpu.VMEM((EXP_EB, EXP_TM), BF16), pltpu.VMEM((EXP_EB, EXP_TM), BF16),
                        pltpu.VMEM((2, EXP_EB, EXP_TM), F32)],
        compiler_params=_cparams(("arbitrary", "arbitrary")),
        name="experts",
    )(xt, u_bf, vt_bf, vt_bf, th4, e14, s2, e2)


def _final_kernel(ht_ref, ft_ref, g_ref, y_ref):
    h = ht_ref[...] + ft_ref[...]
    ms = jnp.mean(h * h, axis=0, keepdims=True)
    y = (h * lax.rsqrt(ms + EPS)) * g_ref[...]
    y_ref[...] = y.T


def _final(ht, ft, ft_block0, g_col, tm):
    t = ht.shape[1]
    return pl.pallas_call(
        _final_kernel,
        grid=(t // tm,),
        in_specs=[
            pl.BlockSpec((D_MODEL, tm), lambda i: (0, i)),
            pl.BlockSpec((D_MODEL, tm), lambda i: (0, ft_block0 + i)),
            pl.BlockSpec((D_MODEL, 1), lambda i: (0, 0)),
        ],
        out_specs=pl.BlockSpec((tm, D_MODEL), lambda i: (i, 0)),
        out_shape=jax.ShapeDtypeStruct((t, D_MODEL), F32),
        compiler_params=_cparams(("arbitrary",)),
        name="final",
    )(ht, ft, g_col)


def kernel(x_prompt, x_sample, state_a_k, state_a_v, state_b_k, state_b_v, g_attn, w_in, attn_sink,
           g_out_a, g_out_b, w_out, g_ffn, peer_wq, peer_k1, peer_k2, peer_u, peer_v, g_final):
    bsz, seq, _ = x_prompt.shape
    db, dt, _ = x_sample.shape
    assert w_in.shape[0] == 1 and dt == 1
    past = PAST_LEN
    lb = state_b_k.shape[2]
    assert lb == max(w for w, _ in DILATED)

    w_in_b = w_in[0].astype(BF16)
    w_out_b = w_out[0].astype(BF16)
    wq_t = peer_wq[0].reshape(D_MODEL, PEER_HEADS * PEER_DKEY).T.astype(BF16)
    k1_b = peer_k1[0].astype(BF16)
    k2_b = peer_k2[0].astype(BF16)
    u_b = peer_u[0].astype(BF16)
    vt_b = peer_v[0].T.astype(BF16)

    rope_flags = jnp.concatenate([
        jnp.ones((1, WIDTH_A + N_KV_A * HEAD_DIM), F32), jnp.zeros((1, N_KV_A * HEAD_DIM), F32),
        jnp.ones((1, 2 * WIDTH_B), F32), jnp.zeros((1, WIDTH_B), F32)], axis=-1)
    cos_p, sin_p = _rope_tables(jnp.arange(seq))
    cos_s, sin_s = _rope_tables(jnp.full((db,), past))

    tm_p = 512
    xp = x_prompt.reshape(bsz * seq, D_MODEL)
    pp = _proj(xp, g_attn, w_in_b, cos_p, sin_p, rope_flags, tm_p, seq // tm_p)
    pp3 = pp.reshape(bsz, seq, QKV_COLS)
    oa_p = _attn_a(pp3, attn_sink[0])
    ob_p = _attn_b(pp3)
    ht_p, xt_p = _merge(oa_p.reshape(bsz * seq, WIDTH_A), ob_p.reshape(bsz * seq, WIDTH_B), xp,
                        g_out_a, g_out_b, w_out_b, g_ffn, 256)

    xs = x_sample.reshape(db, D_MODEL)
    ps = _proj(xs, g_attn, w_in_b, cos_s, sin_s, rope_flags, db, 1)
    c_ka, c_va, c_qb, c_kb, c_vb = (COL_KA * LANES, COL_VA * LANES, COL_QB * LANES,
                                    COL_KB * LANES, COL_VB * LANES)
    qa_s = ps[:, :c_ka].reshape(db, N_HEADS_A, HEAD_DIM)
    ka_s = ps[:, c_ka:c_va].reshape(db, N_KV_A, HEAD_DIM)
    va_s = ps[:, c_va:c_qb].reshape(db, N_KV_A, HEAD_DIM)
    kb_s = ps[:, c_kb:c_vb]
    vb_s = ps[:, c_vb:]
    kt_a = jnp.transpose(state_a_k[0], (0, 2, 3, 1))
    vt_a = jnp.transpose(state_a_v[0], (0, 2, 3, 1))
    skt_b = jnp.transpose(state_b_k[0], (0, 2, 3, 1))
    svt_b = jnp.transpose(state_b_v[0], (0, 2, 3, 1))
    oa_s = _samp_a(qa_s, kt_a, vt_a, ka_s, va_s, attn_sink[0].reshape(N_KV_A, G_A, 1))
    obt_s = _samp_b(skt_b, svt_b, ps[:, c_qb:c_kb].T, kb_s.T, vb_s.T)
    ht_s, xt_s = _merge(oa_s.reshape(db, WIDTH_A), obt_s.T, xs, g_out_a, g_out_b, w_out_b, g_ffn, db)

    t_all = bsz * seq + db
    t_pad = -(-t_all // EXP_TM) * EXP_TM
    xt = jnp.concatenate([xt_p, xt_s, jnp.zeros((D_MODEL, t_pad - t_all), BF16)], axis=1)
    th, e1, s2, e2 = _router(xt, wq_t, k1_b, k2_b)
    ft = _experts(xt, u_b, vt_b, th, e1, s2, e2)

    g_col = g_final.reshape(D_MODEL, 1)
    y_p = _final(ht_p, ft, 0, g_col, 256).reshape(bsz, seq, D_MODEL)
    y_s = _final(ht_s, ft, (bsz * seq) // db, g_col, db).reshape(db, 1, D_MODEL)

    rows_a = min(WINDOW_A, seq)
    ka_p = pp3[:, seq - rows_a:, c_ka:c_va].reshape(1, bsz, rows_a, N_KV_A, HEAD_DIM)
    va_p = pp3[:, seq - rows_a:, c_va:c_qb].reshape(1, bsz, rows_a, N_KV_A, HEAD_DIM)
    kb_p = pp3[:, :, c_kb:c_vb].reshape(1, bsz, seq, N_HEADS_B, HEAD_DIM)
    vb_p = pp3[:, :, c_vb:].reshape(1, bsz, seq, N_HEADS_B, HEAD_DIM)
    return (y_p, y_s, ka_p, va_p, kb_p, vb_p,
            ka_s.reshape(1, db, 1, N_KV_A, HEAD_DIM), va_s.reshape(1, db, 1, N_KV_A, HEAD_DIM),
            kb_s.reshape(1, db, 1, N_HEADS_B, HEAD_DIM), vb_s.reshape(1, db, 1, N_HEADS_B, HEAD_DIM))
```

```python
import functools
import math

import jax
import jax.numpy as jnp
import numpy as np
from jax import lax
from jax.experimental import pallas as pl
from jax.experimental.pallas import tpu as pltpu

D_MODEL = 4096
HEAD_DIM = 64
WIDTH_A = 2048
WIDTH_B = 2048
N_HEADS_A = 32
N_KV_A = 4
G_A = 8
N_HEADS_B = 32
WINDOW_A = 128
DILATED = ((128, 1), (512, 4), (2048, 16))
BLOCK = 128
PAST_LEN = 8192
ROPE_THETA = 10000.0
SCALE = HEAD_DIM ** -0.5
PEER_HEADS = 8
PEER_NKEYS = 128
PEER_EXPERTS = PEER_NKEYS * PEER_NKEYS
PEER_DKEY = 256
PEER_TOPK = 16
EPS = 1e-6
QKV_COLS = WIDTH_A + 2 * N_KV_A * HEAD_DIM + 3 * WIDTH_B

COL_QA = 0
COL_KA = WIDTH_A // 128
COL_VA = COL_KA + N_KV_A * HEAD_DIM // 128
COL_QB = COL_VA + N_KV_A * HEAD_DIM // 128
COL_KB = COL_QB + WIDTH_B // 128
COL_VB = COL_KB + WIDTH_B // 128

LANES = 128
VMEM_LIMIT = 60 * 1024 * 1024

BF16 = jnp.bfloat16
F32 = jnp.float32
NEG_INF = float("-inf")


def _cparams(sem):
    return pltpu.CompilerParams(dimension_semantics=sem, vmem_limit_bytes=VMEM_LIMIT)


PROJ_TN = 512


def _proj_kernel(x_ref, g_ref, w_ref, cos_ref, sin_ref, rope_ref, o_ref, xn_ref):
    j = pl.program_id(1)

    @pl.when(j == 0)
    def _():
        x = x_ref[...]
        ms = jnp.mean(x * x, axis=-1, keepdims=True)
        xn_ref[...] = ((x * lax.rsqrt(ms + EPS)) * g_ref[...]).astype(BF16)

    p = jnp.dot(xn_ref[...], w_ref[...], preferred_element_type=F32)

    lane = lax.broadcasted_iota(jnp.int32, (1, LANES), 1)
    first_half = (lane % HEAD_DIM) < (HEAD_DIM // 2)
    cos = cos_ref[...]
    sin = sin_ref[...]
    for c in range(PROJ_TN // LANES):
        sl = slice(c * LANES, (c + 1) * LANES)
        pc = p[:, sl]
        partner = jnp.where(first_half, pltpu.roll(pc, LANES - HEAD_DIM // 2, 1),
                            pltpu.roll(pc, HEAD_DIM // 2, 1))
        roped = pc * cos + partner * sin
        o_ref[:, sl] = jnp.where(rope_ref[:, sl] > 0.0, roped, pc)


def _proj(x2d, g, w_bf, cos_t, sin_t, rope_flags, tm, pos_blocks):
    t = x2d.shape[0]
    grid = (t // tm, QKV_COLS // PROJ_TN)
    return pl.pallas_call(
        _proj_kernel,
        grid=grid,
        in_specs=[
            pl.BlockSpec((tm, D_MODEL), lambda i, j: (i, 0), pipeline_mode=pl.Buffered(1)),
            pl.BlockSpec((1, D_MODEL), lambda i, j: (0, 0)),
            pl.BlockSpec((D_MODEL, PROJ_TN), lambda i, j: (0, j)),
            pl.BlockSpec((tm, LANES), lambda i, j: (i % pos_blocks, 0)),
            pl.BlockSpec((tm, LANES), lambda i, j: (i % pos_blocks, 0)),
            pl.BlockSpec((1, PROJ_TN), lambda i, j: (0, j)),
        ],
        out_specs=pl.BlockSpec((tm, PROJ_TN), lambda i, j: (i, j)),
        out_shape=jax.ShapeDtypeStruct((t, QKV_COLS), F32),
        scratch_shapes=[pltpu.VMEM((tm, D_MODEL), BF16)],
        compiler_params=_cparams(("arbitrary", "arbitrary")),
        name="proj",
    )(x2d, g, w_bf, cos_t, sin_t, rope_flags)


def _rope_tables(pos):
    half = HEAD_DIM // 2
    inv = ROPE_THETA ** (-jnp.arange(half, dtype=F32) / half)
    ang = pos.astype(F32)[:, None] * inv[None, :]
    cos = jnp.cos(ang)
    sin = jnp.sin(ang)
    cos_t = jnp.concatenate([cos, cos, cos, cos], axis=-1)
    sin_t = jnp.concatenate([-sin, sin, -sin, sin], axis=-1)
    return cos_t, sin_t


def _attn_a_kernel(sink_ref, q_ref, kp_ref, kc_ref, vp_ref, vc_ref, o_ref):
    kp_id = pl.program_id(1)
    n = pl.program_id(2)
    lane = lax.broadcasted_iota(jnp.int32, (1, LANES), 1)
    lo = lane < HEAD_DIM
    qi = lax.broadcasted_iota(jnp.int32, (BLOCK, 2 * BLOCK), 0)
    kj = lax.broadcasted_iota(jnp.int32, (BLOCK, 2 * BLOCK), 1)
    first_key = jnp.where(n == 0, BLOCK, 0)
    valid = (kj >= qi) & (kj <= qi + WINDOW_A) & (kj >= first_key)
    k2 = jnp.concatenate([kp_ref[...], kc_ref[...]], axis=0)
    v2 = jnp.concatenate([vp_ref[...], vc_ref[...]], axis=0)
    nt = (((1,), (1,)), ((), ()))
    for kvl in range(2):
        if kvl == 0:
            k_lo = jnp.where(lo, k2, 0.0)
            k_hi = pltpu.roll(k_lo, HEAD_DIM, 1)
            v_lo = jnp.where(lo, v2, 0.0)
            v_hi = pltpu.roll(v_lo, HEAD_DIM, 1)
        else:
            k_hi = jnp.where(lo, 0.0, k2)
            k_lo = pltpu.roll(k_hi, HEAD_DIM, 1)
            v_hi = jnp.where(lo, 0.0, v2)
            v_lo = pltpu.roll(v_hi, HEAD_DIM, 1)
        kb = (k_lo.astype(BF16), k_hi.astype(BF16))
        vb = (v_lo.astype(BF16), v_hi.astype(BF16))
        scores = []
        for c in range(G_A // 2):
            col = (kvl * (G_A // 2) + c) * LANES
            qc = q_ref[:, col:col + LANES].astype(BF16)
            for par in range(2):
                scores.append(lax.dot_general(qc, kb[par], nt, preferred_element_type=F32) * SCALE)
        probs = []
        for i, s in enumerate(scores):
            sk = sink_ref[kp_id * (2 * G_A) + kvl * G_A + i]
            s = jnp.where(valid, s, NEG_INF)
            m = jnp.maximum(jnp.max(s, axis=-1, keepdims=True), sk)
            p = jnp.exp(s - m)
            den = jnp.sum(p, axis=-1, keepdims=True) + jnp.exp(sk - m)
            probs.append((p.astype(BF16), 1.0 / den))
        for c in range(G_A // 2):
            col = (kvl * (G_A // 2) + c) * LANES
            acc = jnp.zeros((BLOCK, LANES), F32)
            for par in range(2):
                p, rden = probs[2 * c + par]
                acc = acc + jnp.dot(p, vb[par], preferred_element_type=F32) * rden
            o_ref[:, col:col + LANES] = acc


def _attn_a(p3, sink):
    b, s, _ = p3.shape
    nb = s // BLOCK
    qw = 2 * G_A * HEAD_DIM
    grid = (b, N_KV_A // 2, nb)
    prev = lambda n: jnp.maximum(n - 1, 0)
    return pl.pallas_call(
        _attn_a_kernel,
        grid=grid,
        in_specs=[
            pl.BlockSpec(memory_space=pltpu.SMEM),
            pl.BlockSpec((None, BLOCK, qw), lambda bi, kp, n: (bi, n, kp)),
            pl.BlockSpec((None, BLOCK, LANES), lambda bi, kp, n: (bi, prev(n), COL_KA + kp)),
            pl.BlockSpec((None, BLOCK, LANES), lambda bi, kp, n: (bi, n, COL_KA + kp)),
            pl.BlockSpec((None, BLOCK, LANES), lambda bi, kp, n: (bi, prev(n), COL_VA + kp)),
            pl.BlockSpec((None, BLOCK, LANES), lambda bi, kp, n: (bi, n, COL_VA + kp)),
        ],
        out_specs=pl.BlockSpec((None, BLOCK, qw), lambda bi, kp, n: (bi, n, kp)),
        out_shape=jax.ShapeDtypeStruct((b, s, WIDTH_A), F32),
        compiler_params=_cparams(("arbitrary", "arbitrary", "arbitrary")),
        name="attn_a",
    )(sink, p3, p3, p3, p3, p3)


def _attn_b_kernel(q_ref, k_ref, v_ref, o_ref, oc_ref, lc_ref):
    seq = q_ref.shape[0]
    lane = lax.broadcasted_iota(jnp.int32, (1, LANES), 1)
    lo = lane < HEAD_DIM
    nt = (((1,), (1,)), ((), ()))

    masks = (lo, jnp.logical_not(lo))

    def blocks(cfg, d, starts, has_prev):
        nk = 2 * BLOCK if has_prev else BLOCK
        qi = lax.broadcasted_iota(jnp.int32, (BLOCK, nk), 0)
        kj = lax.broadcasted_iota(jnp.int32, (BLOCK, nk), 1)
        if has_prev:
            valid = (kj >= qi) & (kj <= qi + BLOCK)
        else:
            valid = kj <= qi
        rows_qs, vs, scores = [], [], []
        for start in starts:
            rows_q = pl.ds(start, BLOCK, stride=d) if d > 1 else pl.ds(start, BLOCK)
            kstart = start - BLOCK * d if has_prev else start
            rows_k = pl.ds(kstart, nk, stride=d) if d > 1 else pl.ds(kstart, nk)
            q = q_ref[rows_q, :].astype(BF16)
            k = k_ref[rows_k, :]
            rows_qs.append(rows_q)
            vs.append(v_ref[rows_k, :])
            for msk in masks:
                kpar = jnp.where(msk, k, 0.0).astype(BF16)
                scores.append(lax.dot_general(q, kpar, nt, preferred_element_type=F32) * SCALE)
        probs = []
        for s in scores:
            s = jnp.where(valid, s, NEG_INF)
            m = jnp.max(s, axis=-1, keepdims=True)
            p = jnp.exp(s - m)
            den = jnp.sum(p, axis=-1, keepdims=True)
            probs.append((p.astype(BF16), 1.0 / den, jnp.log(den) + m))
        for b, (rows_q, v) in enumerate(zip(rows_qs, vs)):
            acc = jnp.zeros((BLOCK, LANES), F32)
            lse = jnp.zeros((BLOCK, LANES), F32)
            for par, msk in enumerate(masks):
                p, rden, l = probs[2 * b + par]
                vpar = jnp.where(msk, v, 0.0).astype(BF16)
                acc = acc + jnp.dot(p, vpar, preferred_element_type=F32) * rden
                lse = jnp.where(msk, l, lse)
            oc_ref[cfg, rows_q, :] = acc
            lc_ref[cfg, rows_q, :] = lse

    def block(cfg, d, start, has_prev):
        blocks(cfg, d, [start], has_prev)

    for cfg, (w, d) in enumerate(DILATED):
        assert w // d == BLOCK
        nblk = seq // d // BLOCK
        span = BLOCK * d
        if d == 1:
            block(cfg, d, 0, False)
            cnt = nblk - 1
            group = next(gs for gs in (5, 4, 3, 2, 1) if cnt % gs == 0)

            def body_seq(it, carry, cfg=cfg, d=d, span=span, group=group):
                blocks(cfg, d, [(1 + it * group + u) * span for u in range(group)], True)
                return carry

            lax.fori_loop(0, cnt // group, body_seq, 0)
            continue
        cg = min(d, 4)
        ncg = d // cg

        def body_first(it, carry, cfg=cfg, d=d, cg=cg):
            blocks(cfg, d, [it * cg + u for u in range(cg)], False)
            return carry

        if ncg == 1:
            body_first(0, 0)
        else:
            lax.fori_loop(0, ncg, body_first, 0)
        if nblk > 1:
            def body_rest(it, carry, cfg=cfg, d=d, cg=cg, ncg=ncg, span=span):
                n = 1 + it // ncg
                rc = it % ncg
                blocks(cfg, d, [rc * cg + u + n * span for u in range(cg)], True)
                return carry

            lax.fori_loop(0, (nblk - 1) * ncg, body_rest, 0)

    def combine(i, carry):
        rows = pl.ds(pl.multiple_of(i * BLOCK, BLOCK), BLOCK)
        l0 = lc_ref[0, rows, :]
        l1 = lc_ref[1, rows, :]
        l2 = lc_ref[2, rows, :]
        mx = jnp.maximum(jnp.maximum(l0, l1), l2)
        w0 = jnp.exp(l0 - mx)
        w1 = jnp.exp(l1 - mx)
        w2 = jnp.exp(l2 - mx)
        num = w0 * oc_ref[0, rows, :] + w1 * oc_ref[1, rows, :] + w2 * oc_ref[2, rows, :]
        o_ref[rows, :] = num / (w0 + w1 + w2)
        return carry

    lax.fori_loop(0, seq // BLOCK, combine, 0)


def _attn_b(p3):
    b, s, _ = p3.shape
    grid = (b, N_HEADS_B // 2)
    ncfg = len(DILATED)
    return pl.pallas_call(
        _attn_b_kernel,
        grid=grid,
        in_specs=[
            pl.BlockSpec((None, s, LANES), lambda bi, hp: (bi, 0, COL_QB + hp)),
            pl.BlockSpec((None, s, LANES), lambda bi, hp: (bi, 0, COL_KB + hp)),
            pl.BlockSpec((None, s, LANES), lambda bi, hp: (bi, 0, COL_VB + hp)),
        ],
        out_specs=pl.BlockSpec((None, s, LANES), lambda bi, hp: (bi, 0, hp)),
        out_shape=jax.ShapeDtypeStruct((b, s, WIDTH_B), F32),
        scratch_shapes=[pltpu.VMEM((ncfg, s, LANES), F32), pltpu.VMEM((ncfg, s, LANES), F32)],
        compiler_params=_cparams(("arbitrary", "arbitrary")),
        name="attn_b",
    )(p3, p3, p3)


def _samp_a_kernel(q_ref, kt_ref, vt_ref, kn_ref, vn_ref, sink_ref, o_ref):
    nt = (((1,), (1,)), ((), ()))
    for kv in range(N_KV_A):
        q = q_ref[0, kv * G_A:(kv + 1) * G_A, :]
        kt = kt_ref[0, kv]
        vt = vt_ref[0, kv]
        kn = kn_ref[0, kv:kv + 1, :]
        vn = vn_ref[0, kv:kv + 1, :]
        sk = sink_ref[kv]
        s = jnp.dot(q.astype(BF16), kt.astype(BF16), preferred_element_type=F32) * SCALE
        sn = jnp.sum(q * kn, axis=-1, keepdims=True) * SCALE
        m = jnp.maximum(jnp.maximum(jnp.max(s, axis=-1, keepdims=True), sn), sk)
        p = jnp.exp(s - m)
        pn = jnp.exp(sn - m)
        den = jnp.sum(p, axis=-1, keepdims=True) + pn + jnp.exp(sk - m)
        o = lax.dot_general(p.astype(BF16), vt.astype(BF16), nt, preferred_element_type=F32)
        o_ref[0, kv * G_A:(kv + 1) * G_A, :] = (o + pn * vn) / den


def _samp_a(qa_s, kt_a, vt_a, kn, vn, sink3):
    db = qa_s.shape[0]
    lb = kt_a.shape[-1]
    assert lb <= WINDOW_A
    return pl.pallas_call(
        _samp_a_kernel,
        grid=(db,),
        in_specs=[
            pl.BlockSpec((1, N_HEADS_A, HEAD_DIM), lambda b: (b, 0, 0)),
            pl.BlockSpec((1, N_KV_A, HEAD_DIM, lb), lambda b: (b, 0, 0, 0)),
            pl.BlockSpec((1, N_KV_A, HEAD_DIM, lb), lambda b: (b, 0, 0, 0)),
            pl.BlockSpec((1, N_KV_A, HEAD_DIM), lambda b: (b, 0, 0)),
            pl.BlockSpec((1, N_KV_A, HEAD_DIM), lambda b: (b, 0, 0)),
            pl.BlockSpec((N_KV_A, G_A, 1), lambda b: (0, 0, 0)),
        ],
        out_specs=pl.BlockSpec((1, N_HEADS_A, HEAD_DIM), lambda b: (b, 0, 0)),
        out_shape=jax.ShapeDtypeStruct((db, N_HEADS_A, HEAD_DIM), F32),
        compiler_params=_cparams(("arbitrary",)),
        name="samp_a",
    )(qa_s, kt_a, vt_a, kn, vn, sink3)


SAMP_B_HG = 8


def _samp_b_kernel(kt_ref, vt_ref, qt_ref, knt_ref, vnt_ref, o_ref):
    b = pl.program_id(1)
    lb = kt_ref.shape[-1]
    laneb = lax.broadcasted_iota(jnp.int32, (1, LANES), 1) == b

    def column(ref):
        return jnp.sum(jnp.where(laneb, ref[...], 0.0), axis=-1, keepdims=True)

    qcol = column(qt_ref)
    kncol = column(knt_ref)
    vncol = column(vnt_ref)
    hrow = lax.broadcasted_iota(jnp.int32, (SAMP_B_HG, 1), 0)
    s = jnp.zeros((SAMP_B_HG, lb), F32)
    sn = jnp.zeros((SAMP_B_HG, 1), F32)
    for h in range(SAMP_B_HG):
        hs = slice(h * HEAD_DIM, (h + 1) * HEAD_DIM)
        s_h = jnp.sum(kt_ref[0, h] * qcol[hs], axis=0, keepdims=True)
        sn_h = jnp.sum(qcol[hs] * kncol[hs], axis=0, keepdims=True)
        s = jnp.where(hrow == h, s_h, s)
        sn = jnp.where(hrow == h, sn_h, sn)
    s = s * SCALE
    sn = sn * SCALE
    dist = lb - lax.broadcasted_iota(jnp.int32, (1, lb), 1)
    ps, pns, lses = [], [], []
    for w, d in DILATED:
        valid = ((dist % d) == 0) & (dist <= w)
        m = jnp.maximum(jnp.max(jnp.where(valid, s, NEG_INF), axis=-1, keepdims=True), sn)
        p = jnp.where(valid, jnp.exp(s - m), 0.0)
        pn = jnp.exp(sn - m)
        den = jnp.sum(p, axis=-1, keepdims=True) + pn
        ps.append(p / den)
        pns.append(pn / den)
        lses.append(jnp.log(den) + m)
    mx = functools.reduce(jnp.maximum, lses)
    ws = [jnp.exp(l - mx) for l in lses]
    wsum = functools.reduce(lambda a, c: a + c, ws)
    pmix = functools.reduce(lambda a, c: a + c, [w_ * p_ for w_, p_ in zip(ws, ps)]) / wsum
    pnmix = functools.reduce(lambda a, c: a + c, [w_ * p_ for w_, p_ in zip(ws, pns)]) / wsum
    cols = []
    for h in range(SAMP_B_HG):
        hs = slice(h * HEAD_DIM, (h + 1) * HEAD_DIM)
        oc = jnp.sum(vt_ref[0, h] * pmix[h:h + 1, :], axis=-1, keepdims=True)
        cols.append(oc + pnmix[h:h + 1, :] * vncol[hs])
    ocol = jnp.concatenate(cols, axis=0)

    @pl.when(b == 0)
    def _():
        o_ref[...] = jnp.zeros_like(o_ref)

    o_ref[...] = jnp.where(laneb, ocol, o_ref[...])


def _samp_b(kt_b, vt_b, qt, knt, vnt):
    db, nh, hd, lb = kt_b.shape
    assert db == LANES
    rows = SAMP_B_HG * HEAD_DIM
    grid = (nh // SAMP_B_HG, db)
    return pl.pallas_call(
        _samp_b_kernel,
        grid=grid,
        in_specs=[
            pl.BlockSpec((1, SAMP_B_HG, hd, lb), lambda g, b: (b, g, 0, 0)),
            pl.BlockSpec((1, SAMP_B_HG, hd, lb), lambda g, b: (b, g, 0, 0)),
            pl.BlockSpec((rows, db), lambda g, b: (g, 0)),
            pl.BlockSpec((rows, db), lambda g, b: (g, 0)),
            pl.BlockSpec((rows, db), lambda g, b: (g, 0)),
        ],
        out_specs=pl.BlockSpec((rows, db), lambda g, b: (g, 0)),
        out_shape=jax.ShapeDtypeStruct((nh * hd, db), F32),
        compiler_params=_cparams(("arbitrary", "arbitrary")),
        name="samp_b",
    )(kt_b, vt_b, qt, knt, vnt)


MERGE_TN = 1024


MERGE_NORM_ROWS = 512


def _merge_kernel(oa_ref, ob_ref, x_ref, ga_ref, gb_ref, w_ref, gfc_ref, ht_ref, xt_ref, cat_ref, ssq_ref):
    j = pl.program_id(1)
    nj = pl.num_programs(1)

    @pl.when(j == 0)
    def _():
        oa = oa_ref[...]
        ob = ob_ref[...]
        ya = (oa * lax.rsqrt(jnp.mean(oa * oa, axis=-1, keepdims=True) + EPS)) * ga_ref[...]
        yb = (ob * lax.rsqrt(jnp.mean(ob * ob, axis=-1, keepdims=True) + EPS)) * gb_ref[...]
        cat_ref[:, :WIDTH_A] = ya.astype(BF16)
        cat_ref[:, WIDTH_A:] = yb.astype(BF16)
        ssq_ref[...] = jnp.zeros_like(ssq_ref)

    h_t = (x_ref[...] + jnp.dot(cat_ref[...], w_ref[...], preferred_element_type=F32)).T
    ht_ref[pl.ds(pl.multiple_of(j * MERGE_TN, MERGE_TN), MERGE_TN), :] = h_t
    ssq_ref[...] += jnp.sum(h_t * h_t, axis=0, keepdims=True)

    @pl.when(j == nj - 1)
    def _():
        rinv = lax.rsqrt(ssq_ref[...] * (1.0 / D_MODEL) + EPS)
        for c in range(D_MODEL // MERGE_NORM_ROWS):
            rs = slice(c * MERGE_NORM_ROWS, (c + 1) * MERGE_NORM_ROWS)
            xt_ref[rs, :] = ((ht_ref[rs, :] * rinv) * gfc_ref[rs, :]).astype(BF16)


def _merge(oa, ob, x2d, ga, gb, w_bf, gf_col, tm):
    t = x2d.shape[0]
    nj = D_MODEL // MERGE_TN
    grid = (t // tm, nj)
    once = pl.Buffered(1)
    return pl.pallas_call(
        _merge_kernel,
        grid=grid,
        in_specs=[
            pl.BlockSpec((tm, WIDTH_A), lambda i, j: (i, 0), pipeline_mode=once),
            pl.BlockSpec((tm, WIDTH_B), lambda i, j: (i, 0), pipeline_mode=once),
            pl.BlockSpec((tm, MERGE_TN), lambda i, j: (i, j)),
            pl.BlockSpec((1, WIDTH_A), lambda i, j: (0, 0)),
            pl.BlockSpec((1, WIDTH_B), lambda i, j: (0, 0)),
            pl.BlockSpec((D_MODEL, MERGE_TN), lambda i, j: (0, j)),
            pl.BlockSpec((D_MODEL, 1), lambda i, j: (0, 0), pipeline_mode=once),
        ],
        out_specs=[
            pl.BlockSpec((D_MODEL, tm), lambda i, j: (0, i), pipeline_mode=once),
            pl.BlockSpec((D_MODEL, tm), lambda i, j: (0, i), pipeline_mode=once),
        ],
        out_shape=[
            jax.ShapeDtypeStruct((D_MODEL, t), F32),
            jax.ShapeDtypeStruct((D_MODEL, t), BF16),
        ],
        scratch_shapes=[pltpu.VMEM((tm, D_MODEL), BF16), pltpu.VMEM((1, tm), F32)],
        compiler_params=_cparams(("arbitrary", "arbitrary")),
        name="merge",
    )(oa, ob, x2d, ga, gb, w_bf, gf_col)


ROUTER_TM = 512


def _top_values(x, k):
    row = lax.broadcasted_iota(jnp.int32, (k, x.shape[1]), 0)
    vals = jnp.zeros((k, x.shape[1]), F32)
    cur = x
    for r in range(k):
        m = jnp.max(cur, axis=0, keepdims=True)
        vals = jnp.where(row == r, m, vals)
        if r + 1 < k:
            cur = jnp.where(cur == m, NEG_INF, cur)
    return vals


def _router_kernel(x_ref, wq_ref, k1_ref, k2_ref, th_ref, e1_ref, s2_ref, e2_ref):
    half = PEER_DKEY // 2
    qt = jnp.dot(wq_ref[...], x_ref[...], preferred_element_type=F32)
    s1_all = jnp.dot(k1_ref[0], qt[:half].astype(BF16), preferred_element_type=F32)
    s2_all = jnp.dot(k2_ref[0], qt[half:].astype(BF16), preferred_element_type=F32)
    for c in range(ROUTER_TM // LANES):
        sl = slice(c * LANES, (c + 1) * LANES)
        s1 = s1_all[:, sl]
        s2 = s2_all[:, sl]
        v1 = _top_values(s1, PEER_TOPK)
        v2 = _top_values(s2, PEER_TOPK)
        cands = [v1[0:1] + v2[0:8], v1[0:1] + v2[8:16]]
        cands += [v1[a:a + 1] + v2[0:8] for a in range(1, 8)]
        cands += [v1[8:16] + v2[0:1]]
        cand = jnp.concatenate(cands, axis=0)
        tk = _top_values(cand, PEER_TOPK)[PEER_TOPK - 1:PEER_TOPK]
        m1 = v1[0:1]
        m2 = v2[0:1]
        z = jnp.sum(jnp.where(cand >= tk, jnp.exp(cand - (m1 + m2)), 0.0), axis=0, keepdims=True)
        theta = jnp.full(s1.shape, jnp.inf, F32)
        for b in range(PEER_TOPK):
            vb = v2[b:b + 1]
            theta = jnp.where((s1 + vb) >= tk, vb, theta)
        theta = jnp.where(s1 >= v1[PEER_TOPK - 1:PEER_TOPK], theta, jnp.inf)
        th_ref[0, :, sl] = theta
        e1_ref[0, :, sl] = jnp.exp(s1 - m1) / z
        s2_ref[0, :, sl] = s2
        e2_ref[0, :, sl] = jnp.exp(s2 - m2)


def _router(xt, wqt, k1b, k2b):
    t = xt.shape[1]
    grid = (t // ROUTER_TM, PEER_HEADS)
    out = jax.ShapeDtypeStruct((PEER_HEADS, PEER_NKEYS, t), F32)
    ospec = pl.BlockSpec((1, PEER_NKEYS, ROUTER_TM), lambda i, h: (h, 0, i))
    return pl.pallas_call(
        _router_kernel,
        grid=grid,
        in_specs=[
            pl.BlockSpec((D_MODEL, ROUTER_TM), lambda i, h: (0, i)),
            pl.BlockSpec((PEER_DKEY, D_MODEL), lambda i, h: (h, 0)),
            pl.BlockSpec((1, PEER_NKEYS, PEER_DKEY // 2), lambda i, h: (h, 0, 0)),
            pl.BlockSpec((1, PEER_NKEYS, PEER_DKEY // 2), lambda i, h: (h, 0, 0)),
        ],
        out_specs=[ospec, ospec, ospec, ospec],
        out_shape=[out, out, out, out],
        compiler_params=_cparams(("arbitrary", "arbitrary")),
        name="router",
    )(xt, wqt, k1b, k2b)


EXP_TM = 512
EXP_EB = 1024
EXP_IB = EXP_EB // PEER_NKEYS
EXP_UP_ROWS = 512


def _experts_kernel(x_ref, u_ref, vt_ref, th_ref, e1_ref, s2_ref, e2_ref, o_ref, at_ref):
    e = pl.program_id(1)

    @pl.when(e == 0)
    def _():
        o_ref[...] = jnp.zeros_like(o_ref)

    for sub in range(EXP_EB // EXP_UP_ROWS):
        ur = slice(sub * EXP_UP_ROWS, (sub + 1) * EXP_UP_ROWS)
        ht = jnp.dot(u_ref[ur, :], x_ref[...], preferred_element_type=F32)
        for iu in range(EXP_UP_ROWS // PEER_NKEYS):
            il = sub * (EXP_UP_ROWS // PEER_NKEYS) + iu
            hs = slice(iu * PEER_NKEYS, (iu + 1) * PEER_NKEYS)
            rs = slice(il * PEER_NKEYS, (il + 1) * PEER_NKEYS)
            for c in range(EXP_TM // LANES):
                sl = slice(c * LANES, (c + 1) * LANES)
                g = jnp.zeros((PEER_NKEYS, LANES), F32)
                for h in range(PEER_HEADS):
                    th = th_ref[h, 0, il:il + 1, sl]
                    e1 = e1_ref[h, 0, il:il + 1, sl]
                    g = g + jnp.where(s2_ref[h, :, sl] >= th, e2_ref[h, :, sl], 0.0) * e1
                at_ref[rs, sl] = (jax.nn.gelu(ht[hs, sl]) * g).astype(BF16)
    o_ref[...] += jnp.dot(vt_ref[...], at_ref[...], preferred_element_type=F32)


def _experts(xt, u_bf, vt_bf, th, e1, s2, e2):
    t = xt.shape[1]
    nblk = PEER_EXPERTS // EXP_EB
    grid = (t // EXP_TM, nblk)
    th4 = th.reshape(PEER_HEADS, nblk, EXP_IB, t)
    e14 = e1.reshape(PEER_HEADS, nblk, EXP_IB, t)
    once = pl.Buffered(1)
    return pl.pallas_call(
        _experts_kernel,
        grid=grid,
        in_specs=[
            pl.BlockSpec((D_MODEL, EXP_TM), lambda i, e: (0, i), pipeline_mode=once),
            pl.BlockSpec((EXP_EB, D_MODEL), lambda i, e: (e, 0)),
            pl.BlockSpec((D_MODEL, EXP_EB), lambda i, e: (0, e)),
            pl.BlockSpec((PEER_HEADS, 1, EXP_IB, EXP_TM), lambda i, e: (0, e, 0, i)),
            pl.BlockSpec((PEER_HEADS, 1, EXP_IB, EXP_TM), lambda i, e: (0, e, 0, i)),
            pl.BlockSpec((PEER_HEADS, PEER_NKEYS, EXP_TM), lambda i, e: (0, 0, i), pipeline_mode=once),
            pl.BlockSpec((PEER_HEADS, PEER_NKEYS, EXP_TM), lambda i, e: (0, 0, i), pipeline_mode=once),
        ],
        out_specs=pl.BlockSpec((D_MODEL, EXP_TM), lambda i, e: (0, i), pipeline_mode=once),
        out_shape=jax.ShapeDtypeStruct((D_MODEL, t), F32),
        scratch_shapes=[pltpu.VMEM((EXP_EB, EXP_TM), BF16)],
        compiler_params=_cparams(("arbitrary", "arbitrary")),
        name="experts",
    )(xt, u_bf, vt_bf, th4, e14, s2, e2)


def _final_kernel(ht_ref, ft_ref, g_ref, y_ref):
    h = ht_ref[...] + ft_ref[...]
    ms = jnp.mean(h * h, axis=0, keepdims=True)
    y = (h * lax.rsqrt(ms + EPS)) * g_ref[...]
    y_ref[...] = y.T


def _final(ht, ft, ft_block0, g_col, tm):
    t = ht.shape[1]
    return pl.pallas_call(
        _final_kernel,
        grid=(t // tm,),
        in_specs=[
            pl.BlockSpec((D_MODEL, tm), lambda i: (0, i)),
            pl.BlockSpec((D_MODEL, tm), lambda i: (0, ft_block0 + i)),
            pl.BlockSpec((D_MODEL, 1), lambda i: (0, 0)),
        ],
        out_specs=pl.BlockSpec((tm, D_MODEL), lambda i: (i, 0)),
        out_shape=jax.ShapeDtypeStruct((t, D_MODEL), F32),
        compiler_params=_cparams(("arbitrary",)),
        name="final",
    )(ht, ft, g_col)


def kernel(x_prompt, x_sample, state_a_k, state_a_v, state_b_k, state_b_v, g_attn, w_in, attn_sink,
           g_out_a, g_out_b, w_out, g_ffn, peer_wq, peer_k1, peer_k2, peer_u, peer_v, g_final):
    bsz, seq, _ = x_prompt.shape
    db, dt, _ = x_sample.shape
    assert w_in.shape[0] == 1 and dt == 1
    past = PAST_LEN
    lb = state_b_k.shape[2]
    assert lb == max(w for w, _ in DILATED)

    w_in_b = w_in[0].astype(BF16)
    w_out_b = w_out[0].astype(BF16)
    wq_t = peer_wq[0].reshape(D_MODEL, PEER_HEADS * PEER_DKEY).T.astype(BF16)
    k1_b = peer_k1[0].astype(BF16)
    k2_b = peer_k2[0].astype(BF16)
    u_b = peer_u[0].astype(BF16)
    vt_b = peer_v[0].T.astype(BF16)

    rope_flags = jnp.concatenate([
        jnp.ones((1, WIDTH_A + N_KV_A * HEAD_DIM), F32), jnp.zeros((1, N_KV_A * HEAD_DIM), F32),
        jnp.ones((1, 2 * WIDTH_B), F32), jnp.zeros((1, WIDTH_B), F32)], axis=-1)
    cos_p, sin_p = _rope_tables(jnp.arange(seq))
    cos_s, sin_s = _rope_tables(jnp.full((db,), past))

    tm_proj, tm_merge = 1024, 512
    gf_col = g_ffn.reshape(D_MODEL, 1)
    xp = x_prompt.reshape(bsz * seq, D_MODEL)
    pp = _proj(xp, g_attn, w_in_b, cos_p, sin_p, rope_flags, tm_proj, seq // tm_proj)
    pp3 = pp.reshape(bsz, seq, QKV_COLS)
    oa_p = _attn_a(pp3, attn_sink[0])
    ob_p = _attn_b(pp3)
    ht_p, xt_p = _merge(oa_p.reshape(bsz * seq, WIDTH_A), ob_p.reshape(bsz * seq, WIDTH_B), xp,
                        g_out_a, g_out_b, w_out_b, gf_col, tm_merge)

    xs = x_sample.reshape(db, D_MODEL)
    ps = _proj(xs, g_attn, w_in_b, cos_s, sin_s, rope_flags, db, 1)
    c_ka, c_va, c_qb, c_kb, c_vb = (COL_KA * LANES, COL_VA * LANES, COL_QB * LANES,
                                    COL_KB * LANES, COL_VB * LANES)
    qa_s = ps[:, :c_ka].reshape(db, N_HEADS_A, HEAD_DIM)
    ka_s = ps[:, c_ka:c_va].reshape(db, N_KV_A, HEAD_DIM)
    va_s = ps[:, c_va:c_qb].reshape(db, N_KV_A, HEAD_DIM)
    kb_s = ps[:, c_kb:c_vb]
    vb_s = ps[:, c_vb:]
    kt_a = jnp.transpose(state_a_k[0], (0, 2, 3, 1))
    vt_a = jnp.transpose(state_a_v[0], (0, 2, 3, 1))
    skt_b = jnp.transpose(state_b_k[0], (0, 2, 3, 1))
    svt_b = jnp.transpose(state_b_v[0], (0, 2, 3, 1))
    oa_s = _samp_a(qa_s, kt_a, vt_a, ka_s, va_s, attn_sink[0].reshape(N_KV_A, G_A, 1))
    obt_s = _samp_b(skt_b, svt_b, ps[:, c_qb:c_kb].T, kb_s.T, vb_s.T)
    ht_s, xt_s = _merge(oa_s.reshape(db, WIDTH_A), obt_s.T, xs, g_out_a, g_out_b, w_out_b, gf_col, db)

    t_all = bsz * seq + db
    t_pad = -(-t_all // EXP_TM) * EXP_TM
    xt = jnp.concatenate([xt_p, xt_s, jnp.zeros((D_MODEL, t_pad - t_all), BF16)], axis=1)
    th, e1, s2, e2 = _router(xt, wq_t, k1_b, k2_b)
    ft = _experts(xt, u_b, vt_b, th, e1, s2, e2)

    g_col = g_final.reshape(D_MODEL, 1)
    y_p = _final(ht_p, ft, 0, g_col, 256).reshape(bsz, seq, D_MODEL)
    y_s = _final(ht_s, ft, (bsz * seq) // db, g_col, db).reshape(db, 1, D_MODEL)

    rows_a = min(WINDOW_A, seq)
    ka_p = pp3[:, seq - rows_a:, c_ka:c_va].reshape(1, bsz, rows_a, N_KV_A, HEAD_DIM)
    va_p = pp3[:, seq - rows_a:, c_va:c_qb].reshape(1, bsz, rows_a, N_KV_A, HEAD_DIM)
    kb_p = pp3[:, :, c_kb:c_vb].reshape(1, bsz, seq, N_HEADS_B, HEAD_DIM)
    vb_p = pp3[:, :, c_vb:].reshape(1, bsz, seq, N_HEADS_B, HEAD_DIM)
    return (y_p, y_s, ka_p, va_p, kb_p, vb_p,
            ka_s.reshape(1, db, 1, N_KV_A, HEAD_DIM), va_s.reshape(1, db, 1, N_KV_A, HEAD_DIM),
            kb_s.reshape(1, db, 1, N_HEADS_B, HEAD_DIM), vb_s.reshape(1, db, 1, N_HEADS_B, HEAD_DIM))
```

```python
import functools
import math

import jax
import jax.numpy as jnp
import numpy as np
from jax import lax
from jax.experimental import pallas as pl
from jax.experimental.pallas import tpu as pltpu

D_MODEL = 4096
HEAD_DIM = 64
WIDTH_A = 2048
WIDTH_B = 2048
N_HEADS_A = 32
N_KV_A = 4
G_A = 8
N_HEADS_B = 32
WINDOW_A = 128
DILATED = ((128, 1), (512, 4), (2048, 16))
BLOCK = 128
PAST_LEN = 8192
ROPE_THETA = 10000.0
SCALE = HEAD_DIM ** -0.5
PEER_HEADS = 8
PEER_NKEYS = 128
PEER_EXPERTS = PEER_NKEYS * PEER_NKEYS
PEER_DKEY = 256
PEER_TOPK = 16
EPS = 1e-6
QKV_COLS = WIDTH_A + 2 * N_KV_A * HEAD_DIM + 3 * WIDTH_B

COL_QA = 0
COL_KA = WIDTH_A // 128
COL_VA = COL_KA + N_KV_A * HEAD_DIM // 128
COL_QB = COL_VA + N_KV_A * HEAD_DIM // 128
COL_KB = COL_QB + WIDTH_B // 128
COL_VB = COL_KB + WIDTH_B // 128

LANES = 128
VMEM_LIMIT = 60 * 1024 * 1024

BF16 = jnp.bfloat16
F32 = jnp.float32
NEG_INF = float("-inf")


def _cparams(sem):
    return pltpu.CompilerParams(dimension_semantics=sem, vmem_limit_bytes=VMEM_LIMIT)


PROJ_TN = 512


def _proj_kernel(x_ref, g_ref, w_ref, cos_ref, sin_ref, rope_ref, o_ref, xn_ref):
    j = pl.program_id(1)

    @pl.when(j == 0)
    def _():
        x = x_ref[...]
        ms = jnp.mean(x * x, axis=-1, keepdims=True)
        xn_ref[...] = ((x * lax.rsqrt(ms + EPS)) * g_ref[...]).astype(BF16)

    p = jnp.dot(xn_ref[...], w_ref[...], preferred_element_type=F32)

    lane = lax.broadcasted_iota(jnp.int32, (1, LANES), 1)
    first_half = (lane % HEAD_DIM) < (HEAD_DIM // 2)
    cos = cos_ref[...]
    sin = sin_ref[...]
    for c in range(PROJ_TN // LANES):
        sl = slice(c * LANES, (c + 1) * LANES)
        pc = p[:, sl]
        partner = jnp.where(first_half, pltpu.roll(pc, LANES - HEAD_DIM // 2, 1),
                            pltpu.roll(pc, HEAD_DIM // 2, 1))
        roped = pc * cos + partner * sin
        o_ref[c] = jnp.where(rope_ref[:, sl] > 0.0, roped, pc)


def _proj(x2d, g, w_bf, cos_t, sin_t, rope_flags, tm, pos_blocks):
    t = x2d.shape[0]
    grid = (t // tm, QKV_COLS // PROJ_TN)
    return pl.pallas_call(
        _proj_kernel,
        grid=grid,
        in_specs=[
            pl.BlockSpec((tm, D_MODEL), lambda i, j: (i, 0), pipeline_mode=pl.Buffered(1)),
            pl.BlockSpec((1, D_MODEL), lambda i, j: (0, 0)),
            pl.BlockSpec((D_MODEL, PROJ_TN), lambda i, j: (0, j)),
            pl.BlockSpec((tm, LANES), lambda i, j: (i % pos_blocks, 0)),
            pl.BlockSpec((tm, LANES), lambda i, j: (i % pos_blocks, 0)),
            pl.BlockSpec((1, PROJ_TN), lambda i, j: (0, j)),
        ],
        out_specs=pl.BlockSpec((PROJ_TN // LANES, tm, LANES), lambda i, j: (j, i, 0)),
        out_shape=jax.ShapeDtypeStruct((QKV_COLS // LANES, t, LANES), F32),
        scratch_shapes=[pltpu.VMEM((tm, D_MODEL), BF16)],
        compiler_params=_cparams(("arbitrary", "arbitrary")),
        name="proj",
    )(x2d, g, w_bf, cos_t, sin_t, rope_flags)


def _rope_tables(pos):
    half = HEAD_DIM // 2
    inv = ROPE_THETA ** (-jnp.arange(half, dtype=F32) / half)
    ang = pos.astype(F32)[:, None] * inv[None, :]
    cos = jnp.cos(ang)
    sin = jnp.sin(ang)
    cos_t = jnp.concatenate([cos, cos, cos, cos], axis=-1)
    sin_t = jnp.concatenate([-sin, sin, -sin, sin], axis=-1)
    return cos_t, sin_t


def _attn_a_kernel(sink_ref, q_ref, kp_ref, kc_ref, vp_ref, vc_ref, o_ref):
    kp_id = pl.program_id(1)
    n = pl.program_id(2)
    lane = lax.broadcasted_iota(jnp.int32, (1, LANES), 1)
    lo = lane < HEAD_DIM
    qi = lax.broadcasted_iota(jnp.int32, (BLOCK, 2 * BLOCK), 0)
    kj = lax.broadcasted_iota(jnp.int32, (BLOCK, 2 * BLOCK), 1)
    first_key = jnp.where(n == 0, BLOCK, 0)
    valid = (kj >= qi) & (kj <= qi + WINDOW_A) & (kj >= first_key)
    k2 = jnp.concatenate([kp_ref[...], kc_ref[...]], axis=0)
    v2 = jnp.concatenate([vp_ref[...], vc_ref[...]], axis=0)
    nt = (((1,), (1,)), ((), ()))
    for kvl in range(2):
        if kvl == 0:
            k_lo = jnp.where(lo, k2, 0.0)
            k_hi = pltpu.roll(k_lo, HEAD_DIM, 1)
            v_lo = jnp.where(lo, v2, 0.0)
            v_hi = pltpu.roll(v_lo, HEAD_DIM, 1)
        else:
            k_hi = jnp.where(lo, 0.0, k2)
            k_lo = pltpu.roll(k_hi, HEAD_DIM, 1)
            v_hi = jnp.where(lo, 0.0, v2)
            v_lo = pltpu.roll(v_hi, HEAD_DIM, 1)
        kb = (k_lo.astype(BF16), k_hi.astype(BF16))
        vb = (v_lo.astype(BF16), v_hi.astype(BF16))
        scores = []
        for c in range(G_A // 2):
            qc = q_ref[kvl * (G_A // 2) + c].astype(BF16)
            for par in range(2):
                scores.append(lax.dot_general(qc, kb[par], nt, preferred_element_type=F32) * SCALE)
        probs = []
        for i, s in enumerate(scores):
            sk = sink_ref[kp_id * (2 * G_A) + kvl * G_A + i]
            s = jnp.where(valid, s, NEG_INF)
            m = jnp.maximum(jnp.max(s, axis=-1, keepdims=True), sk)
            p = jnp.exp(s - m)
            den = jnp.sum(p, axis=-1, keepdims=True) + jnp.exp(sk - m)
            probs.append((p.astype(BF16), 1.0 / den))
        for c in range(G_A // 2):
            col = (kvl * (G_A // 2) + c) * LANES
            acc = jnp.zeros((BLOCK, LANES), F32)
            for par in range(2):
                p, rden = probs[2 * c + par]
                acc = acc + jnp.dot(p, vb[par], preferred_element_type=F32) * rden
            o_ref[:, col:col + LANES] = acc


def _attn_a(p4, sink):
    _, b, s, _ = p4.shape
    nb = s // BLOCK
    qb = 2 * G_A * HEAD_DIM // LANES
    grid = (b, N_KV_A // 2, nb)
    prev = lambda n: jnp.maximum(n - 1, 0)
    return pl.pallas_call(
        _attn_a_kernel,
        grid=grid,
        in_specs=[
            pl.BlockSpec(memory_space=pltpu.SMEM),
            pl.BlockSpec((qb, None, BLOCK, LANES), lambda bi, kp, n: (kp, bi, n, 0)),
            pl.BlockSpec((None, None, BLOCK, LANES), lambda bi, kp, n: (COL_KA + kp, bi, prev(n), 0)),
            pl.BlockSpec((None, None, BLOCK, LANES), lambda bi, kp, n: (COL_KA + kp, bi, n, 0)),
            pl.BlockSpec((None, None, BLOCK, LANES), lambda bi, kp, n: (COL_VA + kp, bi, prev(n), 0)),
            pl.BlockSpec((None, None, BLOCK, LANES), lambda bi, kp, n: (COL_VA + kp, bi, n, 0)),
        ],
        out_specs=pl.BlockSpec((None, BLOCK, qb * LANES), lambda bi, kp, n: (bi, n, kp)),
        out_shape=jax.ShapeDtypeStruct((b, s, WIDTH_A), F32),
        compiler_params=_cparams(("arbitrary", "arbitrary", "arbitrary")),
        name="attn_a",
    )(sink, p4, p4, p4, p4, p4)


def _attn_b_kernel(q_ref, k_ref, v_ref, o_ref, oc_ref, lc_ref):
    seq = q_ref.shape[0]
    lane = lax.broadcasted_iota(jnp.int32, (1, LANES), 1)
    lo = lane < HEAD_DIM
    nt = (((1,), (1,)), ((), ()))

    masks = (lo, jnp.logical_not(lo))

    def blocks(cfg, d, starts, has_prev):
        nk = 2 * BLOCK if has_prev else BLOCK
        qi = lax.broadcasted_iota(jnp.int32, (BLOCK, nk), 0)
        kj = lax.broadcasted_iota(jnp.int32, (BLOCK, nk), 1)
        if has_prev:
            valid = (kj >= qi) & (kj <= qi + BLOCK)
        else:
            valid = kj <= qi
        rows_qs, vs, scores = [], [], []
        for start in starts:
            rows_q = pl.ds(start, BLOCK, stride=d) if d > 1 else pl.ds(start, BLOCK)
            kstart = start - BLOCK * d if has_prev else start
            rows_k = pl.ds(kstart, nk, stride=d) if d > 1 else pl.ds(kstart, nk)
            q = q_ref[rows_q, :].astype(BF16)
            k = k_ref[rows_k, :]
            rows_qs.append(rows_q)
            vs.append(v_ref[rows_k, :])
            for msk in masks:
                kpar = jnp.where(msk, k, 0.0).astype(BF16)
                scores.append(lax.dot_general(q, kpar, nt, preferred_element_type=F32) * SCALE)
        probs = []
        for s in scores:
            s = jnp.where(valid, s, NEG_INF)
            m = jnp.max(s, axis=-1, keepdims=True)
            p = jnp.exp(s - m)
            den = jnp.sum(p, axis=-1, keepdims=True)
            probs.append((p.astype(BF16), 1.0 / den, jnp.log(den) + m))
        for b, (rows_q, v) in enumerate(zip(rows_qs, vs)):
            acc = jnp.zeros((BLOCK, LANES), F32)
            lse = jnp.zeros((BLOCK, LANES), F32)
            for par, msk in enumerate(masks):
                p, rden, l = probs[2 * b + par]
                vpar = jnp.where(msk, v, 0.0).astype(BF16)
                acc = acc + jnp.dot(p, vpar, preferred_element_type=F32) * rden
                lse = jnp.where(msk, l, lse)
            oc_ref[cfg, rows_q, :] = acc
            lc_ref[cfg, rows_q, :] = lse

    def block(cfg, d, start, has_prev):
        blocks(cfg, d, [start], has_prev)

    for cfg, (w, d) in enumerate(DILATED):
        assert w // d == BLOCK
        nblk = seq // d // BLOCK
        span = BLOCK * d
        if d == 1:
            block(cfg, d, 0, False)
            cnt = nblk - 1
            group = next(gs for gs in (5, 4, 3, 2, 1) if cnt % gs == 0)

            def body_seq(it, carry, cfg=cfg, d=d, span=span, group=group):
                blocks(cfg, d, [(1 + it * group + u) * span for u in range(group)], True)
                return carry

            lax.fori_loop(0, cnt // group, body_seq, 0)
            continue
        cg = min(d, 4)
        ncg = d // cg

        def body_first(it, carry, cfg=cfg, d=d, cg=cg):
            blocks(cfg, d, [it * cg + u for u in range(cg)], False)
            return carry

        if ncg == 1:
            body_first(0, 0)
        else:
            lax.fori_loop(0, ncg, body_first, 0)
        if nblk > 1:
            def body_rest(it, carry, cfg=cfg, d=d, cg=cg, ncg=ncg, span=span):
                n = 1 + it // ncg
                rc = it % ncg
                blocks(cfg, d, [rc * cg + u + n * span for u in range(cg)], True)
                return carry

            lax.fori_loop(0, (nblk - 1) * ncg, body_rest, 0)

    def combine(i, carry):
        rows = pl.ds(pl.multiple_of(i * BLOCK, BLOCK), BLOCK)
        l0 = lc_ref[0, rows, :]
        l1 = lc_ref[1, rows, :]
        l2 = lc_ref[2, rows, :]
        mx = jnp.maximum(jnp.maximum(l0, l1), l2)
        w0 = jnp.exp(l0 - mx)
        w1 = jnp.exp(l1 - mx)
        w2 = jnp.exp(l2 - mx)
        num = w0 * oc_ref[0, rows, :] + w1 * oc_ref[1, rows, :] + w2 * oc_ref[2, rows, :]
        o_ref[rows, :] = num / (w0 + w1 + w2)
        return carry

    lax.fori_loop(0, seq // BLOCK, combine, 0)


def _attn_b(p4):
    _, b, s, _ = p4.shape
    grid = (b, N_HEADS_B // 2)
    ncfg = len(DILATED)
    return pl.pallas_call(
        _attn_b_kernel,
        grid=grid,
        in_specs=[
            pl.BlockSpec((None, None, s, LANES), lambda bi, hp: (COL_QB + hp, bi, 0, 0)),
            pl.BlockSpec((None, None, s, LANES), lambda bi, hp: (COL_KB + hp, bi, 0, 0)),
            pl.BlockSpec((None, None, s, LANES), lambda bi, hp: (COL_VB + hp, bi, 0, 0)),
        ],
        out_specs=pl.BlockSpec((None, s, LANES), lambda bi, hp: (bi, 0, hp)),
        out_shape=jax.ShapeDtypeStruct((b, s, WIDTH_B), F32),
        scratch_shapes=[pltpu.VMEM((ncfg, s, LANES), F32), pltpu.VMEM((ncfg, s, LANES), F32)],
        compiler_params=_cparams(("arbitrary", "arbitrary")),
        name="attn_b",
    )(p4, p4, p4)


def _samp_a_kernel(q_ref, kt_ref, vt_ref, kn_ref, vn_ref, sink_ref, o_ref):
    nt = (((1,), (1,)), ((), ()))
    for kv in range(N_KV_A):
        q = q_ref[0, kv * G_A:(kv + 1) * G_A, :]
        kt = kt_ref[0, kv]
        vt = vt_ref[0, kv]
        kn = kn_ref[0, kv:kv + 1, :]
        vn = vn_ref[0, kv:kv + 1, :]
        sk = sink_ref[kv]
        s = jnp.dot(q.astype(BF16), kt.astype(BF16), preferred_element_type=F32) * SCALE
        sn = jnp.sum(q * kn, axis=-1, keepdims=True) * SCALE
        m = jnp.maximum(jnp.maximum(jnp.max(s, axis=-1, keepdims=True), sn), sk)
        p = jnp.exp(s - m)
        pn = jnp.exp(sn - m)
        den = jnp.sum(p, axis=-1, keepdims=True) + pn + jnp.exp(sk - m)
        o = lax.dot_general(p.astype(BF16), vt.astype(BF16), nt, preferred_element_type=F32)
        o_ref[0, kv * G_A:(kv + 1) * G_A, :] = (o + pn * vn) / den


def _samp_a(qa_s, kt_a, vt_a, kn, vn, sink3):
    db = qa_s.shape[0]
    lb = kt_a.shape[-1]
    assert lb <= WINDOW_A
    return pl.pallas_call(
        _samp_a_kernel,
        grid=(db,),
        in_specs=[
            pl.BlockSpec((1, N_HEADS_A, HEAD_DIM), lambda b: (b, 0, 0)),
            pl.BlockSpec((1, N_KV_A, HEAD_DIM, lb), lambda b: (b, 0, 0, 0)),
            pl.BlockSpec((1, N_KV_A, HEAD_DIM, lb), lambda b: (b, 0, 0, 0)),
            pl.BlockSpec((1, N_KV_A, HEAD_DIM), lambda b: (b, 0, 0)),
            pl.BlockSpec((1, N_KV_A, HEAD_DIM), lambda b: (b, 0, 0)),
            pl.BlockSpec((N_KV_A, G_A, 1), lambda b: (0, 0, 0)),
        ],
        out_specs=pl.BlockSpec((1, N_HEADS_A, HEAD_DIM), lambda b: (b, 0, 0)),
        out_shape=jax.ShapeDtypeStruct((db, N_HEADS_A, HEAD_DIM), F32),
        compiler_params=_cparams(("arbitrary",)),
        name="samp_a",
    )(qa_s, kt_a, vt_a, kn, vn, sink3)


SAMP_B_HG = 8


def _samp_b_kernel(kt_ref, vt_ref, qt_ref, knt_ref, vnt_ref, o_ref):
    b = pl.program_id(1)
    lb = kt_ref.shape[-1]
    laneb = lax.broadcasted_iota(jnp.int32, (1, LANES), 1) == b

    def column(ref):
        return jnp.sum(jnp.where(laneb, ref[...], 0.0), axis=-1, keepdims=True)

    qcol = column(qt_ref)
    kncol = column(knt_ref)
    vncol = column(vnt_ref)
    hrow = lax.broadcasted_iota(jnp.int32, (SAMP_B_HG, 1), 0)
    s = jnp.zeros((SAMP_B_HG, lb), F32)
    sn = jnp.zeros((SAMP_B_HG, 1), F32)
    for h in range(SAMP_B_HG):
        hs = slice(h * HEAD_DIM, (h + 1) * HEAD_DIM)
        s_h = jnp.sum(kt_ref[0, h] * qcol[hs], axis=0, keepdims=True)
        sn_h = jnp.sum(qcol[hs] * kncol[hs], axis=0, keepdims=True)
        s = jnp.where(hrow == h, s_h, s)
        sn = jnp.where(hrow == h, sn_h, sn)
    s = s * SCALE
    sn = sn * SCALE
    dist = lb - lax.broadcasted_iota(jnp.int32, (1, lb), 1)
    ps, pns, lses = [], [], []
    for w, d in DILATED:
        valid = ((dist % d) == 0) & (dist <= w)
        m = jnp.maximum(jnp.max(jnp.where(valid, s, NEG_INF), axis=-1, keepdims=True), sn)
        p = jnp.where(valid, jnp.exp(s - m), 0.0)
        pn = jnp.exp(sn - m)
        den = jnp.sum(p, axis=-1, keepdims=True) + pn
        ps.append(p / den)
        pns.append(pn / den)
        lses.append(jnp.log(den) + m)
    mx = functools.reduce(jnp.maximum, lses)
    ws = [jnp.exp(l - mx) for l in lses]
    wsum = functools.reduce(lambda a, c: a + c, ws)
    pmix = functools.reduce(lambda a, c: a + c, [w_ * p_ for w_, p_ in zip(ws, ps)]) / wsum
    pnmix = functools.reduce(lambda a, c: a + c, [w_ * p_ for w_, p_ in zip(ws, pns)]) / wsum
    cols = []
    for h in range(SAMP_B_HG):
        hs = slice(h * HEAD_DIM, (h + 1) * HEAD_DIM)
        oc = jnp.sum(vt_ref[0, h] * pmix[h:h + 1, :], axis=-1, keepdims=True)
        cols.append(oc + pnmix[h:h + 1, :] * vncol[hs])
    ocol = jnp.concatenate(cols, axis=0)

    @pl.when(b == 0)
    def _():
        o_ref[...] = jnp.zeros_like(o_ref)

    o_ref[...] = jnp.where(laneb, ocol, o_ref[...])


def _samp_b(kt_b, vt_b, qt, knt, vnt):
    db, nh, hd, lb = kt_b.shape
    assert db == LANES
    rows = SAMP_B_HG * HEAD_DIM
    grid = (nh // SAMP_B_HG, db)
    return pl.pallas_call(
        _samp_b_kernel,
        grid=grid,
        in_specs=[
            pl.BlockSpec((1, SAMP_B_HG, hd, lb), lambda g, b: (b, g, 0, 0)),
            pl.BlockSpec((1, SAMP_B_HG, hd, lb), lambda g, b: (b, g, 0, 0)),
            pl.BlockSpec((rows, db), lambda g, b: (g, 0)),
            pl.BlockSpec((rows, db), lambda g, b: (g, 0)),
            pl.BlockSpec((rows, db), lambda g, b: (g, 0)),
        ],
        out_specs=pl.BlockSpec((rows, db), lambda g, b: (g, 0)),
        out_shape=jax.ShapeDtypeStruct((nh * hd, db), F32),
        compiler_params=_cparams(("arbitrary", "arbitrary")),
        name="samp_b",
    )(kt_b, vt_b, qt, knt, vnt)


MERGE_TN = 1024
MERGE_NORM_ROWS = 512


def _merge_kernel(oa_ref, ob_ref, x_ref, ga_ref, gb_ref, w_ref, gfc_ref, ht_ref, xt_ref, cat_ref, ssq_ref):
    j = pl.program_id(1)
    nj = pl.num_programs(1)

    @pl.when(j == 0)
    def _():
        oa = oa_ref[...]
        ob = ob_ref[...]
        ya = (oa * lax.rsqrt(jnp.mean(oa * oa, axis=-1, keepdims=True) + EPS)) * ga_ref[...]
        yb = (ob * lax.rsqrt(jnp.mean(ob * ob, axis=-1, keepdims=True) + EPS)) * gb_ref[...]
        cat_ref[:, :WIDTH_A] = ya.astype(BF16)
        cat_ref[:, WIDTH_A:] = yb.astype(BF16)
        ssq_ref[...] = jnp.zeros_like(ssq_ref)

    h_t = (x_ref[...] + jnp.dot(cat_ref[...], w_ref[...], preferred_element_type=F32)).T
    ht_ref[pl.ds(pl.multiple_of(j * MERGE_TN, MERGE_TN), MERGE_TN), :] = h_t
    ssq_ref[...] += jnp.sum(h_t * h_t, axis=0, keepdims=True)

    @pl.when(j == nj - 1)
    def _():
        rinv = lax.rsqrt(ssq_ref[...] * (1.0 / D_MODEL) + EPS)
        for c in range(D_MODEL // MERGE_NORM_ROWS):
            rs = slice(c * MERGE_NORM_ROWS, (c + 1) * MERGE_NORM_ROWS)
            xt_ref[rs, :] = ((ht_ref[rs, :] * rinv) * gfc_ref[rs, :]).astype(BF16)


def _merge(oa, ob, x2d, ga, gb, w_bf, gf_col, tm):
    t = x2d.shape[0]
    nj = D_MODEL // MERGE_TN
    grid = (t // tm, nj)
    once = pl.Buffered(1)
    return pl.pallas_call(
        _merge_kernel,
        grid=grid,
        in_specs=[
            pl.BlockSpec((tm, WIDTH_A), lambda i, j: (i, 0), pipeline_mode=once),
            pl.BlockSpec((tm, WIDTH_B), lambda i, j: (i, 0), pipeline_mode=once),
            pl.BlockSpec((tm, MERGE_TN), lambda i, j: (i, j)),
            pl.BlockSpec((1, WIDTH_A), lambda i, j: (0, 0)),
            pl.BlockSpec((1, WIDTH_B), lambda i, j: (0, 0)),
            pl.BlockSpec((D_MODEL, MERGE_TN), lambda i, j: (0, j)),
            pl.BlockSpec((D_MODEL, 1), lambda i, j: (0, 0), pipeline_mode=once),
        ],
        out_specs=[
            pl.BlockSpec((D_MODEL, tm), lambda i, j: (0, i), pipeline_mode=once),
            pl.BlockSpec((D_MODEL, tm), lambda i, j: (0, i), pipeline_mode=once),
        ],
        out_shape=[
            jax.ShapeDtypeStruct((D_MODEL, t), F32),
            jax.ShapeDtypeStruct((D_MODEL, t), BF16),
        ],
        scratch_shapes=[pltpu.VMEM((tm, D_MODEL), BF16), pltpu.VMEM((1, tm), F32)],
        compiler_params=_cparams(("arbitrary", "arbitrary")),
        name="merge",
    )(oa, ob, x2d, ga, gb, w_bf, gf_col)


ROUTER_TM = 512


def _top_values(x, k):
    row = lax.broadcasted_iota(jnp.int32, (k, x.shape[1]), 0)
    vals = jnp.zeros((k, x.shape[1]), F32)
    cur = x
    for r in range(k):
        m = jnp.max(cur, axis=0, keepdims=True)
        vals = jnp.where(row == r, m, vals)
        if r + 1 < k:
            cur = jnp.where(cur == m, NEG_INF, cur)
    return vals


def _router_kernel(x_ref, wq_ref, k1_ref, k2_ref, th_ref, e1_ref, s2_ref, e2_ref):
    half = PEER_DKEY // 2
    qt = jnp.dot(wq_ref[...], x_ref[...], preferred_element_type=F32)
    s1_all = jnp.dot(k1_ref[0], qt[:half].astype(BF16), preferred_element_type=F32)
    s2_all = jnp.dot(k2_ref[0], qt[half:].astype(BF16), preferred_element_type=F32)
    for c in range(ROUTER_TM // LANES):
        sl = slice(c * LANES, (c + 1) * LANES)
        s1 = s1_all[:, sl]
        s2 = s2_all[:, sl]
        v1 = _top_values(s1, PEER_TOPK)
        v2 = _top_values(s2, PEER_TOPK)
        cands = [v1[0:1] + v2[0:8], v1[0:1] + v2[8:16]]
        cands += [v1[a:a + 1] + v2[0:8] for a in range(1, 8)]
        cands += [v1[8:16] + v2[0:1]]
        cand = jnp.concatenate(cands, axis=0)
        tk = _top_values(cand, PEER_TOPK)[PEER_TOPK - 1:PEER_TOPK]
        m1 = v1[0:1]
        m2 = v2[0:1]
        z = jnp.sum(jnp.where(cand >= tk, jnp.exp(cand - (m1 + m2)), 0.0), axis=0, keepdims=True)
        theta = jnp.full(s1.shape, jnp.inf, F32)
        for b in range(PEER_TOPK):
            vb = v2[b:b + 1]
            theta = jnp.where((s1 + vb) >= tk, vb, theta)
        theta = jnp.where(s1 >= v1[PEER_TOPK - 1:PEER_TOPK], theta, jnp.inf)
        th_ref[0, :, sl] = theta
        e1_ref[0, :, sl] = jnp.exp(s1 - m1) / z
        s2_ref[0, :, sl] = s2
        e2_ref[0, :, sl] = jnp.exp(s2 - m2)


def _router(xt, wqt, k1b, k2b):
    t = xt.shape[1]
    grid = (t // ROUTER_TM, PEER_HEADS)
    out = jax.ShapeDtypeStruct((PEER_HEADS, PEER_NKEYS, t), F32)
    ospec = pl.BlockSpec((1, PEER_NKEYS, ROUTER_TM), lambda i, h: (h, 0, i))
    return pl.pallas_call(
        _router_kernel,
        grid=grid,
        in_specs=[
            pl.BlockSpec((D_MODEL, ROUTER_TM), lambda i, h: (0, i)),
            pl.BlockSpec((PEER_DKEY, D_MODEL), lambda i, h: (h, 0)),
            pl.BlockSpec((1, PEER_NKEYS, PEER_DKEY // 2), lambda i, h: (h, 0, 0)),
            pl.BlockSpec((1, PEER_NKEYS, PEER_DKEY // 2), lambda i, h: (h, 0, 0)),
        ],
        out_specs=[ospec, ospec, ospec, ospec],
        out_shape=[out, out, out, out],
        compiler_params=_cparams(("arbitrary", "arbitrary")),
        name="router",
    )(xt, wqt, k1b, k2b)


EXP_TM = 512
EXP_EB = 512
EXP_IB = EXP_EB // PEER_NKEYS


def _experts_kernel(x_ref, u_ref, vt_ref, th_ref, e1_ref, s2_ref, e2_ref, o_ref, at_ref):
    e = pl.program_id(1)

    @pl.when(e == 0)
    def _():
        o_ref[...] = jnp.zeros_like(o_ref)

    ht = jnp.dot(u_ref[...], x_ref[...], preferred_element_type=F32)
    for il in range(EXP_IB):
        rs = slice(il * PEER_NKEYS, (il + 1) * PEER_NKEYS)
        for c in range(EXP_TM // LANES):
            sl = slice(c * LANES, (c + 1) * LANES)
            g = jnp.zeros((PEER_NKEYS, LANES), F32)
            for h in range(PEER_HEADS):
                th = th_ref[h, 0, il:il + 1, sl]
                e1 = e1_ref[h, 0, il:il + 1, sl]
                g = g + jnp.where(s2_ref[h, :, sl] >= th, e2_ref[h, :, sl], 0.0) * e1
            at_ref[rs, sl] = (jax.nn.gelu(ht[rs, sl]) * g).astype(BF16)
    o_ref[...] += jnp.dot(vt_ref[...], at_ref[...], preferred_element_type=F32)


def _experts(xt, u_bf, vt_bf, th, e1, s2, e2):
    t = xt.shape[1]
    grid = (t // EXP_TM, PEER_EXPERTS // EXP_EB)
    nblk = PEER_NKEYS // EXP_IB
    th4 = th.reshape(PEER_HEADS, nblk, EXP_IB, t)
    e14 = e1.reshape(PEER_HEADS, nblk, EXP_IB, t)
    return pl.pallas_call(
        _experts_kernel,
        grid=grid,
        in_specs=[
            pl.BlockSpec((D_MODEL, EXP_TM), lambda i, e: (0, i)),
            pl.BlockSpec((EXP_EB, D_MODEL), lambda i, e: (e, 0)),
            pl.BlockSpec((D_MODEL, EXP_EB), lambda i, e: (0, e)),
            pl.BlockSpec((PEER_HEADS, 1, EXP_IB, EXP_TM), lambda i, e: (0, e, 0, i)),
            pl.BlockSpec((PEER_HEADS, 1, EXP_IB, EXP_TM), lambda i, e: (0, e, 0, i)),
            pl.BlockSpec((PEER_HEADS, PEER_NKEYS, EXP_TM), lambda i, e: (0, 0, i)),
            pl.BlockSpec((PEER_HEADS, PEER_NKEYS, EXP_TM), lambda i, e: (0, 0, i)),
        ],
        out_specs=pl.BlockSpec((D_MODEL, EXP_TM), lambda i, e: (0, i)),
        out_shape=jax.ShapeDtypeStruct((D_MODEL, t), F32),
        scratch_shapes=[pltpu.VMEM((EXP_EB, EXP_TM), BF16)],
        compiler_params=_cparams(("arbitrary", "arbitrary")),
        name="experts",
    )(xt, u_bf, vt_bf, th4, e14, s2, e2)


def _final_kernel(ht_ref, ft_ref, g_ref, y_ref):
    h = ht_ref[...] + ft_ref[...]
    ms = jnp.mean(h * h, axis=0, keepdims=True)
    y = (h * lax.rsqrt(ms + EPS)) * g_ref[...]
    y_ref[...] = y.T


def _final(ht, ft, ft_block0, g_col, tm):
    t = ht.shape[1]
    return pl.pallas_call(
        _final_kernel,
        grid=(t // tm,),
        in_specs=[
            pl.BlockSpec((D_MODEL, tm), lambda i: (0, i)),
            pl.BlockSpec((D_MODEL, tm), lambda i: (0, ft_block0 + i)),
            pl.BlockSpec((D_MODEL, 1), lambda i: (0, 0)),
        ],
        out_specs=pl.BlockSpec((tm, D_MODEL), lambda i: (i, 0)),
        out_shape=jax.ShapeDtypeStruct((t, D_MODEL), F32),
        compiler_params=_cparams(("arbitrary",)),
        name="final",
    )(ht, ft, g_col)


def kernel(x_prompt, x_sample, state_a_k, state_a_v, state_b_k, state_b_v, g_attn, w_in, attn_sink,
           g_out_a, g_out_b, w_out, g_ffn, peer_wq, peer_k1, peer_k2, peer_u, peer_v, g_final):
    bsz, seq, _ = x_prompt.shape
    db, dt, _ = x_sample.shape
    assert w_in.shape[0] == 1 and dt == 1
    past = PAST_LEN
    lb = state_b_k.shape[2]
    assert lb == max(w for w, _ in DILATED)

    w_in_b = w_in[0].astype(BF16)
    w_out_b = w_out[0].astype(BF16)
    wq_t = peer_wq[0].reshape(D_MODEL, PEER_HEADS * PEER_DKEY).T.astype(BF16)
    k1_b = peer_k1[0].astype(BF16)
    k2_b = peer_k2[0].astype(BF16)
    u_b = peer_u[0].astype(BF16)
    vt_b = peer_v[0].T.astype(BF16)

    rope_flags = jnp.concatenate([
        jnp.ones((1, WIDTH_A + N_KV_A * HEAD_DIM), F32), jnp.zeros((1, N_KV_A * HEAD_DIM), F32),
        jnp.ones((1, 2 * WIDTH_B), F32), jnp.zeros((1, WIDTH_B), F32)], axis=-1)
    cos_p, sin_p = _rope_tables(jnp.arange(seq))
    cos_s, sin_s = _rope_tables(jnp.full((db,), past))

    tm_proj, tm_merge = 1024, 512
    gf_col = g_ffn.reshape(D_MODEL, 1)
    xp = x_prompt.reshape(bsz * seq, D_MODEL)
    pp = _proj(xp, g_attn, w_in_b, cos_p, sin_p, rope_flags, tm_proj, seq // tm_proj)
    pp4 = pp.reshape(QKV_COLS // LANES, bsz, seq, LANES)
    oa_p = _attn_a(pp4, attn_sink[0])
    ob_p = _attn_b(pp4)
    ht_p, xt_p = _merge(oa_p.reshape(bsz * seq, WIDTH_A), ob_p.reshape(bsz * seq, WIDTH_B), xp,
                        g_out_a, g_out_b, w_out_b, gf_col, tm_merge)

    xs = x_sample.reshape(db, D_MODEL)
    ps = _proj(xs, g_attn, w_in_b, cos_s, sin_s, rope_flags, db, 1)

    def sample_cols(lo, hi):
        return jnp.transpose(ps[lo:hi], (1, 0, 2)).reshape(db, (hi - lo) * LANES)

    qa_s = sample_cols(COL_QA, COL_KA).reshape(db, N_HEADS_A, HEAD_DIM)
    ka_s = sample_cols(COL_KA, COL_VA).reshape(db, N_KV_A, HEAD_DIM)
    va_s = sample_cols(COL_VA, COL_QB).reshape(db, N_KV_A, HEAD_DIM)
    kb_s = sample_cols(COL_KB, COL_VB)
    vb_s = sample_cols(COL_VB, QKV_COLS // LANES)
    kt_a = jnp.transpose(state_a_k[0], (0, 2, 3, 1))
    vt_a = jnp.transpose(state_a_v[0], (0, 2, 3, 1))
    skt_b = jnp.transpose(state_b_k[0], (0, 2, 3, 1))
    svt_b = jnp.transpose(state_b_v[0], (0, 2, 3, 1))
    oa_s = _samp_a(qa_s, kt_a, vt_a, ka_s, va_s, attn_sink[0].reshape(N_KV_A, G_A, 1))
    obt_s = _samp_b(skt_b, svt_b, sample_cols(COL_QB, COL_KB).T, kb_s.T, vb_s.T)
    ht_s, xt_s = _merge(oa_s.reshape(db, WIDTH_A), obt_s.T, xs, g_out_a, g_out_b, w_out_b, gf_col, db)

    t_all = bsz * seq + db
    t_pad = -(-t_all // EXP_TM) * EXP_TM
    xt = jnp.concatenate([xt_p, xt_s, jnp.zeros((D_MODEL, t_pad - t_all), BF16)], axis=1)
    th, e1, s2, e2 = _router(xt, wq_t, k1_b, k2_b)
    ft = _experts(xt, u_b, vt_b, th, e1, s2, e2)

    g_col = g_final.reshape(D_MODEL, 1)
    y_p = _final(ht_p, ft, 0, g_col, 256).reshape(bsz, seq, D_MODEL)
    y_s = _final(ht_s, ft, (bsz * seq) // db, g_col, db).reshape(db, 1, D_MODEL)

    def prompt_cols(lo, hi, first_row, heads):
        blk = jnp.transpose(pp4[lo:hi, :, first_row:], (1, 2, 0, 3))
        return blk.reshape(1, bsz, seq - first_row, heads, HEAD_DIM)

    rows_a = min(WINDOW_A, seq)
    ka_p = prompt_cols(COL_KA, COL_VA, seq - rows_a, N_KV_A)
    va_p = prompt_cols(COL_VA, COL_QB, seq - rows_a, N_KV_A)
    kb_p = prompt_cols(COL_KB, COL_VB, 0, N_HEADS_B)
    vb_p = prompt_cols(COL_VB, QKV_COLS // LANES, 0, N_HEADS_B)
    return (y_p, y_s, ka_p, va_p, kb_p, vb_p,
            ka_s.reshape(1, db, 1, N_KV_A, HEAD_DIM), va_s.reshape(1, db, 1, N_KV_A, HEAD_DIM),
            kb_s.reshape(1, db, 1, N_HEADS_B, HEAD_DIM), vb_s.reshape(1, db, 1, N_HEADS_B, HEAD_DIM))
```

```python
import functools
import math

import jax
import jax.numpy as jnp
import numpy as np
from jax import lax
from jax.experimental import pallas as pl
from jax.experimental.pallas import tpu as pltpu

D_MODEL = 4096
HEAD_DIM = 64
WIDTH_A = 2048
WIDTH_B = 2048
N_HEADS_A = 32
N_KV_A = 4
G_A = 8
N_HEADS_B = 32
WINDOW_A = 128
DILATED = ((128, 1), (512, 4), (2048, 16))
BLOCK = 128
PAST_LEN = 8192
ROPE_THETA = 10000.0
SCALE = HEAD_DIM ** -0.5
PEER_HEADS = 8
PEER_NKEYS = 128
PEER_EXPERTS = PEER_NKEYS * PEER_NKEYS
PEER_DKEY = 256
PEER_TOPK = 16
EPS = 1e-6
QKV_COLS = WIDTH_A + 2 * N_KV_A * HEAD_DIM + 3 * WIDTH_B

COL_QA = 0
COL_KA = WIDTH_A // 128
COL_VA = COL_KA + N_KV_A * HEAD_DIM // 128
COL_QB = COL_VA + N_KV_A * HEAD_DIM // 128
COL_KB = COL_QB + WIDTH_B // 128
COL_VB = COL_KB + WIDTH_B // 128

LANES = 128
VMEM_LIMIT = 60 * 1024 * 1024

BF16 = jnp.bfloat16
F32 = jnp.float32
NEG_INF = float("-inf")


def _cparams(sem):
    return pltpu.CompilerParams(dimension_semantics=sem, vmem_limit_bytes=VMEM_LIMIT)


PROJ_TN = 512


def _proj_kernel(x_ref, g_ref, w_ref, cos_ref, sin_ref, rope_ref, o_ref, xn_ref):
    j = pl.program_id(1)

    @pl.when(j == 0)
    def _():
        x = x_ref[...]
        ms = jnp.mean(x * x, axis=-1, keepdims=True)
        xn_ref[...] = ((x * lax.rsqrt(ms + EPS)) * g_ref[...]).astype(BF16)

    p = jnp.dot(xn_ref[...], w_ref[...], preferred_element_type=F32)

    lane = lax.broadcasted_iota(jnp.int32, (1, LANES), 1)
    first_half = (lane % HEAD_DIM) < (HEAD_DIM // 2)
    cos = cos_ref[...]
    sin = sin_ref[...]
    for c in range(PROJ_TN // LANES):
        sl = slice(c * LANES, (c + 1) * LANES)
        pc = p[:, sl]
        partner = jnp.where(first_half, pltpu.roll(pc, LANES - HEAD_DIM // 2, 1),
                            pltpu.roll(pc, HEAD_DIM // 2, 1))
        roped = pc * cos + partner * sin
        o_ref[c] = jnp.where(rope_ref[:, sl] > 0.0, roped, pc)


def _proj(x2d, g, w_bf, cos_t, sin_t, rope_flags, tm, pos_blocks):
    t = x2d.shape[0]
    grid = (t // tm, QKV_COLS // PROJ_TN)
    return pl.pallas_call(
        _proj_kernel,
        grid=grid,
        in_specs=[
            pl.BlockSpec((tm, D_MODEL), lambda i, j: (i, 0), pipeline_mode=pl.Buffered(1)),
            pl.BlockSpec((1, D_MODEL), lambda i, j: (0, 0)),
            pl.BlockSpec((D_MODEL, PROJ_TN), lambda i, j: (0, j)),
            pl.BlockSpec((tm, LANES), lambda i, j: (i % pos_blocks, 0)),
            pl.BlockSpec((tm, LANES), lambda i, j: (i % pos_blocks, 0)),
            pl.BlockSpec((1, PROJ_TN), lambda i, j: (0, j)),
        ],
        out_specs=pl.BlockSpec((PROJ_TN // LANES, tm, LANES), lambda i, j: (j, i, 0)),
        out_shape=jax.ShapeDtypeStruct((QKV_COLS // LANES, t, LANES), F32),
        scratch_shapes=[pltpu.VMEM((tm, D_MODEL), BF16)],
        compiler_params=_cparams(("arbitrary", "arbitrary")),
        name="proj",
    )(x2d, g, w_bf, cos_t, sin_t, rope_flags)


def _rope_tables(pos):
    half = HEAD_DIM // 2
    inv = ROPE_THETA ** (-jnp.arange(half, dtype=F32) / half)
    ang = pos.astype(F32)[:, None] * inv[None, :]
    cos = jnp.cos(ang)
    sin = jnp.sin(ang)
    cos_t = jnp.concatenate([cos, cos, cos, cos], axis=-1)
    sin_t = jnp.concatenate([-sin, sin, -sin, sin], axis=-1)
    return cos_t, sin_t


def _attn_a_kernel(sink_ref, q_ref, kp_ref, kc_ref, vp_ref, vc_ref, o_ref):
    kp_id = pl.program_id(1)
    n = pl.program_id(2)
    lane = lax.broadcasted_iota(jnp.int32, (1, LANES), 1)
    lo = lane < HEAD_DIM
    qi = lax.broadcasted_iota(jnp.int32, (BLOCK, 2 * BLOCK), 0)
    kj = lax.broadcasted_iota(jnp.int32, (BLOCK, 2 * BLOCK), 1)
    first_key = jnp.where(n == 0, BLOCK, 0)
    valid = (kj >= qi) & (kj <= qi + WINDOW_A) & (kj >= first_key)
    k2 = jnp.concatenate([kp_ref[...], kc_ref[...]], axis=0)
    v2 = jnp.concatenate([vp_ref[...], vc_ref[...]], axis=0)
    nt = (((1,), (1,)), ((), ()))
    for kvl in range(2):
        if kvl == 0:
            k_lo = jnp.where(lo, k2, 0.0)
            k_hi = pltpu.roll(k_lo, HEAD_DIM, 1)
            v_lo = jnp.where(lo, v2, 0.0)
            v_hi = pltpu.roll(v_lo, HEAD_DIM, 1)
        else:
            k_hi = jnp.where(lo, 0.0, k2)
            k_lo = pltpu.roll(k_hi, HEAD_DIM, 1)
            v_hi = jnp.where(lo, 0.0, v2)
            v_lo = pltpu.roll(v_hi, HEAD_DIM, 1)
        kb = (k_lo.astype(BF16), k_hi.astype(BF16))
        vb = (v_lo.astype(BF16), v_hi.astype(BF16))
        scores = []
        for c in range(G_A // 2):
            qc = q_ref[kvl * (G_A // 2) + c].astype(BF16)
            for par in range(2):
                scores.append(lax.dot_general(qc, kb[par], nt, preferred_element_type=F32) * SCALE)
        probs = []
        for i, s in enumerate(scores):
            sk = sink_ref[kp_id * (2 * G_A) + kvl * G_A + i]
            s = jnp.where(valid, s, NEG_INF)
            m = jnp.maximum(jnp.max(s, axis=-1, keepdims=True), sk)
            p = jnp.exp(s - m)
            den = jnp.sum(p, axis=-1, keepdims=True) + jnp.exp(sk - m)
            probs.append((p.astype(BF16), 1.0 / den))
        for c in range(G_A // 2):
            col = (kvl * (G_A // 2) + c) * LANES
            acc = jnp.zeros((BLOCK, LANES), F32)
            for par in range(2):
                p, rden = probs[2 * c + par]
                acc = acc + jnp.dot(p, vb[par], preferred_element_type=F32) * rden
            o_ref[:, col:col + LANES] = acc


def _attn_a(p4, sink):
    _, b, s, _ = p4.shape
    nb = s // BLOCK
    qb = 2 * G_A * HEAD_DIM // LANES
    grid = (b, N_KV_A // 2, nb)
    prev = lambda n: jnp.maximum(n - 1, 0)
    return pl.pallas_call(
        _attn_a_kernel,
        grid=grid,
        in_specs=[
            pl.BlockSpec(memory_space=pltpu.SMEM),
            pl.BlockSpec((qb, None, BLOCK, LANES), lambda bi, kp, n: (kp, bi, n, 0)),
            pl.BlockSpec((None, None, BLOCK, LANES), lambda bi, kp, n: (COL_KA + kp, bi, prev(n), 0)),
            pl.BlockSpec((None, None, BLOCK, LANES), lambda bi, kp, n: (COL_KA + kp, bi, n, 0)),
            pl.BlockSpec((None, None, BLOCK, LANES), lambda bi, kp, n: (COL_VA + kp, bi, prev(n), 0)),
            pl.BlockSpec((None, None, BLOCK, LANES), lambda bi, kp, n: (COL_VA + kp, bi, n, 0)),
        ],
        out_specs=pl.BlockSpec((None, BLOCK, qb * LANES), lambda bi, kp, n: (bi, n, kp)),
        out_shape=jax.ShapeDtypeStruct((b, s, WIDTH_A), F32),
        compiler_params=_cparams(("arbitrary", "arbitrary", "arbitrary")),
        name="attn_a",
    )(sink, p4, p4, p4, p4, p4)


def _attn_b_kernel(q_ref, k_ref, v_ref, o_ref, oc_ref, lc_ref):
    seq = q_ref.shape[0]
    lane = lax.broadcasted_iota(jnp.int32, (1, LANES), 1)
    lo = lane < HEAD_DIM
    nt = (((1,), (1,)), ((), ()))

    masks = (lo, jnp.logical_not(lo))

    def blocks(cfg, d, starts, has_prev):
        nk = 2 * BLOCK if has_prev else BLOCK
        qi = lax.broadcasted_iota(jnp.int32, (BLOCK, nk), 0)
        kj = lax.broadcasted_iota(jnp.int32, (BLOCK, nk), 1)
        if has_prev:
            valid = (kj >= qi) & (kj <= qi + BLOCK)
        else:
            valid = kj <= qi
        rows_qs, vs, scores = [], [], []
        for start in starts:
            rows_q = pl.ds(start, BLOCK, stride=d) if d > 1 else pl.ds(start, BLOCK)
            kstart = start - BLOCK * d if has_prev else start
            rows_k = pl.ds(kstart, nk, stride=d) if d > 1 else pl.ds(kstart, nk)
            q = q_ref[rows_q, :].astype(BF16)
            k = k_ref[rows_k, :]
            rows_qs.append(rows_q)
            vs.append(v_ref[rows_k, :])
            for msk in masks:
                kpar = jnp.where(msk, k, 0.0).astype(BF16)
                scores.append(lax.dot_general(q, kpar, nt, preferred_element_type=F32) * SCALE)
        probs = []
        for s in scores:
            s = jnp.where(valid, s, NEG_INF)
            m = jnp.max(s, axis=-1, keepdims=True)
            p = jnp.exp(s - m)
            den = jnp.sum(p, axis=-1, keepdims=True)
            probs.append((p.astype(BF16), 1.0 / den, jnp.log(den) + m))
        for b, (rows_q, v) in enumerate(zip(rows_qs, vs)):
            acc = jnp.zeros((BLOCK, LANES), F32)
            lse = jnp.zeros((BLOCK, LANES), F32)
            for par, msk in enumerate(masks):
                p, rden, l = probs[2 * b + par]
                vpar = jnp.where(msk, v, 0.0).astype(BF16)
                acc = acc + jnp.dot(p, vpar, preferred_element_type=F32) * rden
                lse = jnp.where(msk, l, lse)
            oc_ref[cfg, rows_q, :] = acc
            lc_ref[cfg, rows_q, :] = lse

    def block(cfg, d, start, has_prev):
        blocks(cfg, d, [start], has_prev)

    for cfg, (w, d) in enumerate(DILATED):
        assert w // d == BLOCK
        nblk = seq // d // BLOCK
        span = BLOCK * d
        if d == 1:
            block(cfg, d, 0, False)
            cnt = nblk - 1
            group = next(gs for gs in (5, 4, 3, 2, 1) if cnt % gs == 0)

            def body_seq(it, carry, cfg=cfg, d=d, span=span, group=group):
                blocks(cfg, d, [(1 + it * group + u) * span for u in range(group)], True)
                return carry

            lax.fori_loop(0, cnt // group, body_seq, 0)
            continue
        cg = min(d, 4)
        ncg = d // cg

        def body_first(it, carry, cfg=cfg, d=d, cg=cg):
            blocks(cfg, d, [it * cg + u for u in range(cg)], False)
            return carry

        if ncg == 1:
            body_first(0, 0)
        else:
            lax.fori_loop(0, ncg, body_first, 0)
        if nblk > 1:
            def body_rest(it, carry, cfg=cfg, d=d, cg=cg, ncg=ncg, span=span):
                n = 1 + it // ncg
                rc = it % ncg
                blocks(cfg, d, [rc * cg + u + n * span for u in range(cg)], True)
                return carry

            lax.fori_loop(0, (nblk - 1) * ncg, body_rest, 0)

    def combine(i, carry):
        rows = pl.ds(pl.multiple_of(i * BLOCK, BLOCK), BLOCK)
        l0 = lc_ref[0, rows, :]
        l1 = lc_ref[1, rows, :]
        l2 = lc_ref[2, rows, :]
        mx = jnp.maximum(jnp.maximum(l0, l1), l2)
        w0 = jnp.exp(l0 - mx)
        w1 = jnp.exp(l1 - mx)
        w2 = jnp.exp(l2 - mx)
        num = w0 * oc_ref[0, rows, :] + w1 * oc_ref[1, rows, :] + w2 * oc_ref[2, rows, :]
        o_ref[rows, :] = num / (w0 + w1 + w2)
        return carry

    lax.fori_loop(0, seq // BLOCK, combine, 0)


def _attn_b(p4):
    _, b, s, _ = p4.shape
    grid = (b, N_HEADS_B // 2)
    ncfg = len(DILATED)
    return pl.pallas_call(
        _attn_b_kernel,
        grid=grid,
        in_specs=[
            pl.BlockSpec((None, None, s, LANES), lambda bi, hp: (COL_QB + hp, bi, 0, 0)),
            pl.BlockSpec((None, None, s, LANES), lambda bi, hp: (COL_KB + hp, bi, 0, 0)),
            pl.BlockSpec((None, None, s, LANES), lambda bi, hp: (COL_VB + hp, bi, 0, 0)),
        ],
        out_specs=pl.BlockSpec((None, s, LANES), lambda bi, hp: (bi, 0, hp)),
        out_shape=jax.ShapeDtypeStruct((b, s, WIDTH_B), F32),
        scratch_shapes=[pltpu.VMEM((ncfg, s, LANES), F32), pltpu.VMEM((ncfg, s, LANES), F32)],
        compiler_params=_cparams(("arbitrary", "arbitrary")),
        name="attn_b",
    )(p4, p4, p4)


SAMP_A_BB = 8


def _samp_a_kernel(q_ref, kt_ref, vt_ref, kn_ref, vn_ref, sink_ref, o_ref):
    nt = (((1,), (1,)), ((), ()))
    for bb in range(SAMP_A_BB):
        for kv in range(N_KV_A):
            q = q_ref[bb, kv * G_A:(kv + 1) * G_A, :]
            kt = kt_ref[bb, kv]
            vt = vt_ref[bb, kv]
            kn = kn_ref[bb, kv:kv + 1, :]
            vn = vn_ref[bb, kv:kv + 1, :]
            sk = sink_ref[kv]
            s = jnp.dot(q.astype(BF16), kt.astype(BF16), preferred_element_type=F32) * SCALE
            sn = jnp.sum(q * kn, axis=-1, keepdims=True) * SCALE
            m = jnp.maximum(jnp.maximum(jnp.max(s, axis=-1, keepdims=True), sn), sk)
            p = jnp.exp(s - m)
            pn = jnp.exp(sn - m)
            den = jnp.sum(p, axis=-1, keepdims=True) + pn + jnp.exp(sk - m)
            o = lax.dot_general(p.astype(BF16), vt.astype(BF16), nt, preferred_element_type=F32)
            o_ref[bb, kv * G_A:(kv + 1) * G_A, :] = (o + pn * vn) / den


def _samp_a(qa_s, kt_a, vt_a, kn, vn, sink3):
    db = qa_s.shape[0]
    lb = kt_a.shape[-1]
    assert lb <= WINDOW_A
    bb = SAMP_A_BB
    return pl.pallas_call(
        _samp_a_kernel,
        grid=(db // bb,),
        in_specs=[
            pl.BlockSpec((bb, N_HEADS_A, HEAD_DIM), lambda b: (b, 0, 0)),
            pl.BlockSpec((bb, N_KV_A, HEAD_DIM, lb), lambda b: (b, 0, 0, 0)),
            pl.BlockSpec((bb, N_KV_A, HEAD_DIM, lb), lambda b: (b, 0, 0, 0)),
            pl.BlockSpec((bb, N_KV_A, HEAD_DIM), lambda b: (b, 0, 0)),
            pl.BlockSpec((bb, N_KV_A, HEAD_DIM), lambda b: (b, 0, 0)),
            pl.BlockSpec((N_KV_A, G_A, 1), lambda b: (0, 0, 0)),
        ],
        out_specs=pl.BlockSpec((bb, N_HEADS_A, HEAD_DIM), lambda b: (b, 0, 0)),
        out_shape=jax.ShapeDtypeStruct((db, N_HEADS_A, HEAD_DIM), F32),
        compiler_params=_cparams(("arbitrary",)),
        name="samp_a",
    )(qa_s, kt_a, vt_a, kn, vn, sink3)


SAMP_B_HG = 8


def _samp_b_kernel(kt_ref, vt_ref, qt_ref, knt_ref, vnt_ref, o_ref):
    b = pl.program_id(1)
    lb = kt_ref.shape[-1]
    laneb = lax.broadcasted_iota(jnp.int32, (1, LANES), 1) == b

    def column(ref):
        return jnp.sum(jnp.where(laneb, ref[...], 0.0), axis=-1, keepdims=True)

    qcol = column(qt_ref)
    kncol = column(knt_ref)
    vncol = column(vnt_ref)
    hrow = lax.broadcasted_iota(jnp.int32, (SAMP_B_HG, 1), 0)
    s = jnp.zeros((SAMP_B_HG, lb), F32)
    sn = jnp.zeros((SAMP_B_HG, 1), F32)
    for h in range(SAMP_B_HG):
        hs = slice(h * HEAD_DIM, (h + 1) * HEAD_DIM)
        s_h = jnp.sum(kt_ref[0, h] * qcol[hs], axis=0, keepdims=True)
        sn_h = jnp.sum(qcol[hs] * kncol[hs], axis=0, keepdims=True)
        s = jnp.where(hrow == h, s_h, s)
        sn = jnp.where(hrow == h, sn_h, sn)
    s = s * SCALE
    sn = sn * SCALE
    dist = lb - lax.broadcasted_iota(jnp.int32, (1, lb), 1)
    ps, pns, lses = [], [], []
    for w, d in DILATED:
        valid = ((dist % d) == 0) & (dist <= w)
        m = jnp.maximum(jnp.max(jnp.where(valid, s, NEG_INF), axis=-1, keepdims=True), sn)
        p = jnp.where(valid, jnp.exp(s - m), 0.0)
        pn = jnp.exp(sn - m)
        den = jnp.sum(p, axis=-1, keepdims=True) + pn
        ps.append(p / den)
        pns.append(pn / den)
        lses.append(jnp.log(den) + m)
    mx = functools.reduce(jnp.maximum, lses)
    ws = [jnp.exp(l - mx) for l in lses]
    wsum = functools.reduce(lambda a, c: a + c, ws)
    pmix = functools.reduce(lambda a, c: a + c, [w_ * p_ for w_, p_ in zip(ws, ps)]) / wsum
    pnmix = functools.reduce(lambda a, c: a + c, [w_ * p_ for w_, p_ in zip(ws, pns)]) / wsum
    cols = []
    for h in range(SAMP_B_HG):
        hs = slice(h * HEAD_DIM, (h + 1) * HEAD_DIM)
        oc = jnp.sum(vt_ref[0, h] * pmix[h:h + 1, :], axis=-1, keepdims=True)
        cols.append(oc + pnmix[h:h + 1, :] * vncol[hs])
    ocol = jnp.concatenate(cols, axis=0)

    @pl.when(b == 0)
    def _():
        o_ref[...] = jnp.zeros_like(o_ref)

    o_ref[...] = jnp.where(laneb, ocol, o_ref[...])


def _samp_b(kt_b, vt_b, qt, knt, vnt):
    db, nh, hd, lb = kt_b.shape
    assert db == LANES
    rows = SAMP_B_HG * HEAD_DIM
    grid = (nh // SAMP_B_HG, db)
    return pl.pallas_call(
        _samp_b_kernel,
        grid=grid,
        in_specs=[
            pl.BlockSpec((1, SAMP_B_HG, hd, lb), lambda g, b: (b, g, 0, 0)),
            pl.BlockSpec((1, SAMP_B_HG, hd, lb), lambda g, b: (b, g, 0, 0)),
            pl.BlockSpec((rows, db), lambda g, b: (g, 0)),
            pl.BlockSpec((rows, db), lambda g, b: (g, 0)),
            pl.BlockSpec((rows, db), lambda g, b: (g, 0)),
        ],
        out_specs=pl.BlockSpec((rows, db), lambda g, b: (g, 0)),
        out_shape=jax.ShapeDtypeStruct((nh * hd, db), F32),
        compiler_params=_cparams(("arbitrary", "arbitrary")),
        name="samp_b",
    )(kt_b, vt_b, qt, knt, vnt)


MERGE_TN = 1024
MERGE_NORM_ROWS = 512


def _merge_kernel(oa_ref, ob_ref, x_ref, ga_ref, gb_ref, w_ref, gfc_ref, ht_ref, xt_ref, cat_ref, ssq_ref):
    j = pl.program_id(1)
    nj = pl.num_programs(1)

    @pl.when(j == 0)
    def _():
        oa = oa_ref[...]
        ob = ob_ref[...]
        ya = (oa * lax.rsqrt(jnp.mean(oa * oa, axis=-1, keepdims=True) + EPS)) * ga_ref[...]
        yb = (ob * lax.rsqrt(jnp.mean(ob * ob, axis=-1, keepdims=True) + EPS)) * gb_ref[...]
        cat_ref[:, :WIDTH_A] = ya.astype(BF16)
        cat_ref[:, WIDTH_A:] = yb.astype(BF16)
        ssq_ref[...] = jnp.zeros_like(ssq_ref)

    h_t = (x_ref[...] + jnp.dot(cat_ref[...], w_ref[...], preferred_element_type=F32)).T
    ht_ref[pl.ds(pl.multiple_of(j * MERGE_TN, MERGE_TN), MERGE_TN), :] = h_t
    ssq_ref[...] += jnp.sum(h_t * h_t, axis=0, keepdims=True)

    @pl.when(j == nj - 1)
    def _():
        rinv = lax.rsqrt(ssq_ref[...] * (1.0 / D_MODEL) + EPS)
        for c in range(D_MODEL // MERGE_NORM_ROWS):
            rs = slice(c * MERGE_NORM_ROWS, (c + 1) * MERGE_NORM_ROWS)
            xt_ref[rs, :] = ((ht_ref[rs, :] * rinv) * gfc_ref[rs, :]).astype(BF16)


def _merge(oa, ob, x2d, ga, gb, w_bf, gf_col, tm):
    t = x2d.shape[0]
    nj = D_MODEL // MERGE_TN
    grid = (t // tm, nj)
    once = pl.Buffered(1)
    return pl.pallas_call(
        _merge_kernel,
        grid=grid,
        in_specs=[
            pl.BlockSpec((tm, WIDTH_A), lambda i, j: (i, 0), pipeline_mode=once),
            pl.BlockSpec((tm, WIDTH_B), lambda i, j: (i, 0), pipeline_mode=once),
            pl.BlockSpec((tm, MERGE_TN), lambda i, j: (i, j)),
            pl.BlockSpec((1, WIDTH_A), lambda i, j: (0, 0)),
            pl.BlockSpec((1, WIDTH_B), lambda i, j: (0, 0)),
            pl.BlockSpec((D_MODEL, MERGE_TN), lambda i, j: (0, j)),
            pl.BlockSpec((D_MODEL, 1), lambda i, j: (0, 0), pipeline_mode=once),
        ],
        out_specs=[
            pl.BlockSpec((D_MODEL, tm), lambda i, j: (0, i), pipeline_mode=once),
            pl.BlockSpec((D_MODEL, tm), lambda i, j: (0, i), pipeline_mode=once),
        ],
        out_shape=[
            jax.ShapeDtypeStruct((D_MODEL, t), F32),
            jax.ShapeDtypeStruct((D_MODEL, t), BF16),
        ],
        scratch_shapes=[pltpu.VMEM((tm, D_MODEL), BF16), pltpu.VMEM((1, tm), F32)],
        compiler_params=_cparams(("arbitrary", "arbitrary")),
        name="merge",
    )(oa, ob, x2d, ga, gb, w_bf, gf_col)


ROUTER_TM = 512
GATE_ROWS = 4


def _top_values(x, k):
    row = lax.broadcasted_iota(jnp.int32, (k, x.shape[1]), 0)
    vals = jnp.zeros((k, x.shape[1]), F32)
    cur = x
    for r in range(k):
        m = jnp.max(cur, axis=0, keepdims=True)
        vals = jnp.where(row == r, m, vals)
        if r + 1 < k:
            cur = jnp.where(cur == m, NEG_INF, cur)
    return vals


def _router_kernel(x_ref, wq_ref, k1_ref, k2_ref, th_ref, e1_ref, s2_ref, e2_ref):
    half = PEER_DKEY // 2
    qt = jnp.dot(wq_ref[...], x_ref[...], preferred_element_type=F32)
    s1_all = jnp.dot(k1_ref[0], qt[:half].astype(BF16), preferred_element_type=F32)
    s2_all = jnp.dot(k2_ref[0], qt[half:].astype(BF16), preferred_element_type=F32)
    for c in range(ROUTER_TM // LANES):
        sl = slice(c * LANES, (c + 1) * LANES)
        s1 = s1_all[:, sl]
        s2 = s2_all[:, sl]
        v1 = _top_values(s1, PEER_TOPK)
        v2 = _top_values(s2, PEER_TOPK)
        cands = [v1[0:1] + v2[0:8], v1[0:1] + v2[8:16]]
        cands += [v1[a:a + 1] + v2[0:8] for a in range(1, 8)]
        cands += [v1[8:16] + v2[0:1]]
        cand = jnp.concatenate(cands, axis=0)
        tk = _top_values(cand, PEER_TOPK)[PEER_TOPK - 1:PEER_TOPK]
        m1 = v1[0:1]
        m2 = v2[0:1]
        z = jnp.sum(jnp.where(cand >= tk, jnp.exp(cand - (m1 + m2)), 0.0), axis=0, keepdims=True)
        theta = jnp.full(s1.shape, jnp.inf, F32)
        for b in range(PEER_TOPK):
            vb = v2[b:b + 1]
            theta = jnp.where((s1 + vb) >= tk, vb, theta)
        theta = jnp.where(s1 >= v1[PEER_TOPK - 1:PEER_TOPK], theta, jnp.inf)
        e1 = jnp.exp(s1 - m1) / z
        for grp in range(PEER_NKEYS // GATE_ROWS):
            rows = slice(grp * GATE_ROWS, (grp + 1) * GATE_ROWS)
            th_ref[0, grp, :, sl] = theta[rows]
            e1_ref[0, grp, :, sl] = e1[rows]
        s2_ref[0, :, sl] = s2
        e2_ref[0, :, sl] = jnp.exp(s2 - m2)


def _router(xt, wqt, k1b, k2b):
    t = xt.shape[1]
    grid = (t // ROUTER_TM, PEER_HEADS)
    out = jax.ShapeDtypeStruct((PEER_HEADS, PEER_NKEYS, t), F32)
    ospec = pl.BlockSpec((1, PEER_NKEYS, ROUTER_TM), lambda i, h: (h, 0, i))
    ngrp = PEER_NKEYS // GATE_ROWS
    gout = jax.ShapeDtypeStruct((PEER_HEADS, ngrp, GATE_ROWS, t), F32)
    gspec = pl.BlockSpec((1, ngrp, GATE_ROWS, ROUTER_TM), lambda i, h: (h, 0, 0, i))
    return pl.pallas_call(
        _router_kernel,
        grid=grid,
        in_specs=[
            pl.BlockSpec((D_MODEL, ROUTER_TM), lambda i, h: (0, i)),
            pl.BlockSpec((PEER_DKEY, D_MODEL), lambda i, h: (h, 0)),
            pl.BlockSpec((1, PEER_NKEYS, PEER_DKEY // 2), lambda i, h: (h, 0, 0)),
            pl.BlockSpec((1, PEER_NKEYS, PEER_DKEY // 2), lambda i, h: (h, 0, 0)),
        ],
        out_specs=[gspec, gspec, ospec, ospec],
        out_shape=[gout, gout, out, out],
        compiler_params=_cparams(("arbitrary", "arbitrary")),
        name="router",
    )(xt, wqt, k1b, k2b)


EXP_TM = 512
EXP_EB = 512
EXP_IB = EXP_EB // PEER_NKEYS
assert EXP_IB == GATE_ROWS


def _experts_kernel(x_ref, u_ref, vt_ref, th_ref, e1_ref, s2_ref, e2_ref, o_ref, at_ref):
    e = pl.program_id(1)

    @pl.when(e == 0)
    def _():
        o_ref[...] = jnp.zeros_like(o_ref)

    ht = jnp.dot(u_ref[...], x_ref[...], preferred_element_type=F32)
    for il in range(EXP_IB):
        rs = slice(il * PEER_NKEYS, (il + 1) * PEER_NKEYS)
        for c in range(EXP_TM // LANES):
            sl = slice(c * LANES, (c + 1) * LANES)
            g = jnp.zeros((PEER_NKEYS, LANES), F32)
            for h in range(PEER_HEADS):
                th = th_ref[h, 0, il:il + 1, sl]
                e1 = e1_ref[h, 0, il:il + 1, sl]
                g = g + jnp.where(s2_ref[h, :, sl] >= th, e2_ref[h, :, sl], 0.0) * e1
            at_ref[rs, sl] = (jax.nn.gelu(ht[rs, sl]) * g).astype(BF16)
    o_ref[...] += jnp.dot(vt_ref[...], at_ref[...], preferred_element_type=F32)


def _experts(xt, u_bf, vt_bf, th4, e14, s2, e2):
    t = xt.shape[1]
    grid = (t // EXP_TM, PEER_EXPERTS // EXP_EB)
    return pl.pallas_call(
        _experts_kernel,
        grid=grid,
        in_specs=[
            pl.BlockSpec((D_MODEL, EXP_TM), lambda i, e: (0, i)),
            pl.BlockSpec((EXP_EB, D_MODEL), lambda i, e: (e, 0)),
            pl.BlockSpec((D_MODEL, EXP_EB), lambda i, e: (0, e)),
            pl.BlockSpec((PEER_HEADS, 1, EXP_IB, EXP_TM), lambda i, e: (0, e, 0, i)),
            pl.BlockSpec((PEER_HEADS, 1, EXP_IB, EXP_TM), lambda i, e: (0, e, 0, i)),
            pl.BlockSpec((PEER_HEADS, PEER_NKEYS, EXP_TM), lambda i, e: (0, 0, i)),
            pl.BlockSpec((PEER_HEADS, PEER_NKEYS, EXP_TM), lambda i, e: (0, 0, i)),
        ],
        out_specs=pl.BlockSpec((D_MODEL, EXP_TM), lambda i, e: (0, i)),
        out_shape=jax.ShapeDtypeStruct((D_MODEL, t), F32),
        scratch_shapes=[pltpu.VMEM((EXP_EB, EXP_TM), BF16)],
        compiler_params=_cparams(("arbitrary", "arbitrary")),
        name="experts",
    )(xt, u_bf, vt_bf, th4, e14, s2, e2)


CAST_ROWS = 1024
CAST_COLS = 2048


def _cast_t_kernel(v_ref, o_ref):
    o_ref[...] = v_ref[...].T.astype(BF16)


def _cast_transposed(v):
    e, d = v.shape
    return pl.pallas_call(
        _cast_t_kernel,
        grid=(e // CAST_ROWS, d // CAST_COLS),
        in_specs=[pl.BlockSpec((CAST_ROWS, CAST_COLS), lambda i, j: (i, j))],
        out_specs=pl.BlockSpec((CAST_COLS, CAST_ROWS), lambda i, j: (j, i)),
        out_shape=jax.ShapeDtypeStruct((d, e), BF16),
        compiler_params=_cparams(("arbitrary", "arbitrary")),
        name="cast_t",
    )(v)


def _final_kernel(ht_ref, ft_ref, g_ref, y_ref):
    h = ht_ref[...] + ft_ref[...]
    ms = jnp.mean(h * h, axis=0, keepdims=True)
    y = (h * lax.rsqrt(ms + EPS)) * g_ref[...]
    y_ref[...] = y.T


def _final(ht, ft, ft_block0, g_col, tm):
    t = ht.shape[1]
    return pl.pallas_call(
        _final_kernel,
        grid=(t // tm,),
        in_specs=[
            pl.BlockSpec((D_MODEL, tm), lambda i: (0, i)),
            pl.BlockSpec((D_MODEL, tm), lambda i: (0, ft_block0 + i)),
            pl.BlockSpec((D_MODEL, 1), lambda i: (0, 0)),
        ],
        out_specs=pl.BlockSpec((tm, D_MODEL), lambda i: (i, 0)),
        out_shape=jax.ShapeDtypeStruct((t, D_MODEL), F32),
        compiler_params=_cparams(("arbitrary",)),
        name="final",
    )(ht, ft, g_col)


def kernel(x_prompt, x_sample, state_a_k, state_a_v, state_b_k, state_b_v, g_attn, w_in, attn_sink,
           g_out_a, g_out_b, w_out, g_ffn, peer_wq, peer_k1, peer_k2, peer_u, peer_v, g_final):
    bsz, seq, _ = x_prompt.shape
    db, dt, _ = x_sample.shape
    assert w_in.shape[0] == 1 and dt == 1
    past = PAST_LEN
    lb = state_b_k.shape[2]
    assert lb == max(w for w, _ in DILATED)

    w_in_b = w_in[0].astype(BF16)
    w_out_b = w_out[0].astype(BF16)
    wq_t = peer_wq[0].reshape(D_MODEL, PEER_HEADS * PEER_DKEY).T.astype(BF16)
    k1_b = peer_k1[0].astype(BF16)
    k2_b = peer_k2[0].astype(BF16)
    u_b = peer_u[0].astype(BF16)
    vt_b = _cast_transposed(peer_v[0])

    rope_flags = jnp.concatenate([
        jnp.ones((1, WIDTH_A + N_KV_A * HEAD_DIM), F32), jnp.zeros((1, N_KV_A * HEAD_DIM), F32),
        jnp.ones((1, 2 * WIDTH_B), F32), jnp.zeros((1, WIDTH_B), F32)], axis=-1)
    cos_p, sin_p = _rope_tables(jnp.arange(seq))
    cos_s, sin_s = _rope_tables(jnp.full((db,), past))

    tm_proj, tm_merge = 1024, 512
    gf_col = g_ffn.reshape(D_MODEL, 1)
    xp = x_prompt.reshape(bsz * seq, D_MODEL)
    pp = _proj(xp, g_attn, w_in_b, cos_p, sin_p, rope_flags, tm_proj, seq // tm_proj)
    pp4 = pp.reshape(QKV_COLS // LANES, bsz, seq, LANES)
    oa_p = _attn_a(pp4, attn_sink[0])
    ob_p = _attn_b(pp4)
    ht_p, xt_p = _merge(oa_p.reshape(bsz * seq, WIDTH_A), ob_p.reshape(bsz * seq, WIDTH_B), xp,
                        g_out_a, g_out_b, w_out_b, gf_col, tm_merge)

    xs = x_sample.reshape(db, D_MODEL)
    ps = _proj(xs, g_attn, w_in_b, cos_s, sin_s, rope_flags, db, 1)

    def sample_cols(lo, hi):
        return jnp.transpose(ps[lo:hi], (1, 0, 2)).reshape(db, (hi - lo) * LANES)

    qa_s = sample_cols(COL_QA, COL_KA).reshape(db, N_HEADS_A, HEAD_DIM)
    ka_s = sample_cols(COL_KA, COL_VA).reshape(db, N_KV_A, HEAD_DIM)
    va_s = sample_cols(COL_VA, COL_QB).reshape(db, N_KV_A, HEAD_DIM)
    kb_s = sample_cols(COL_KB, COL_VB)
    vb_s = sample_cols(COL_VB, QKV_COLS // LANES)
    kt_a = jnp.transpose(state_a_k[0], (0, 2, 3, 1))
    vt_a = jnp.transpose(state_a_v[0], (0, 2, 3, 1))
    skt_b = jnp.transpose(state_b_k[0], (0, 2, 3, 1))
    svt_b = jnp.transpose(state_b_v[0], (0, 2, 3, 1))
    oa_s = _samp_a(qa_s, kt_a, vt_a, ka_s, va_s, attn_sink[0].reshape(N_KV_A, G_A, 1))
    obt_s = _samp_b(skt_b, svt_b, sample_cols(COL_QB, COL_KB).T, kb_s.T, vb_s.T)
    ht_s, xt_s = _merge(oa_s.reshape(db, WIDTH_A), obt_s.T, xs, g_out_a, g_out_b, w_out_b, gf_col, db)

    t_all = bsz * seq + db
    t_pad = -(-t_all // EXP_TM) * EXP_TM
    xt = jnp.concatenate([xt_p, xt_s, jnp.zeros((D_MODEL, t_pad - t_all), BF16)], axis=1)
    th, e1, s2, e2 = _router(xt, wq_t, k1_b, k2_b)
    ft = _experts(xt, u_b, vt_b, th, e1, s2, e2)

    g_col = g_final.reshape(D_MODEL, 1)
    y_p = _final(ht_p, ft, 0, g_col, 256).reshape(bsz, seq, D_MODEL)
    y_s = _final(ht_s, ft, (bsz * seq) // db, g_col, db).reshape(db, 1, D_MODEL)

    def prompt_cols(lo, hi, first_row, heads):
        blk = jnp.transpose(pp4[lo:hi, :, first_row:], (1, 2, 0, 3))
        return blk.reshape(1, bsz, seq - first_row, heads, HEAD_DIM)

    rows_a = min(WINDOW_A, seq)
    ka_p = prompt_cols(COL_KA, COL_VA, seq - rows_a, N_KV_A)
    va_p = prompt_cols(COL_VA, COL_QB, seq - rows_a, N_KV_A)
    kb_p = prompt_cols(COL_KB, COL_VB, 0, N_HEADS_B)
    vb_p = prompt_cols(COL_VB, QKV_COLS // LANES, 0, N_HEADS_B)
    return (y_p, y_s, ka_p, va_p, kb_p, vb_p,
            ka_s.reshape(1, db, 1, N_KV_A, HEAD_DIM), va_s.reshape(1, db, 1, N_KV_A, HEAD_DIM),
            kb_s.reshape(1, db, 1, N_HEADS_B, HEAD_DIM), vb_s.reshape(1, db, 1, N_HEADS_B, HEAD_DIM))
```

```python
import functools
import math

import jax
import jax.numpy as jnp
import numpy as np
from jax import lax
from jax.experimental import pallas as pl
from jax.experimental.pallas import tpu as pltpu

D_MODEL = 4096
HEAD_DIM = 64
WIDTH_A = 2048
WIDTH_B = 2048
N_HEADS_A = 32
N_KV_A = 4
G_A = 8
N_HEADS_B = 32
WINDOW_A = 128
DILATED = ((128, 1), (512, 4), (2048, 16))
BLOCK = 128
PAST_LEN = 8192
ROPE_THETA = 10000.0
SCALE = HEAD_DIM ** -0.5
PEER_HEADS = 8
PEER_NKEYS = 128
PEER_EXPERTS = PEER_NKEYS * PEER_NKEYS
PEER_DKEY = 256
PEER_TOPK = 16
EPS = 1e-6
QKV_COLS = WIDTH_A + 2 * N_KV_A * HEAD_DIM + 3 * WIDTH_B

COL_QA = 0
COL_KA = WIDTH_A // 128
COL_VA = COL_KA + N_KV_A * HEAD_DIM // 128
COL_QB = COL_VA + N_KV_A * HEAD_DIM // 128
COL_KB = COL_QB + WIDTH_B // 128
COL_VB = COL_KB + WIDTH_B // 128

LANES = 128
VMEM_LIMIT = 60 * 1024 * 1024

BF16 = jnp.bfloat16
F32 = jnp.float32
NEG_INF = float("-inf")


def _cparams(sem):
    return pltpu.CompilerParams(dimension_semantics=sem, vmem_limit_bytes=VMEM_LIMIT)


PROJ_TN = 512


def _proj_kernel(x_ref, g_ref, w_ref, cos_ref, sin_ref, rope_ref, o_ref, xn_ref):
    j = pl.program_id(1)

    @pl.when(j == 0)
    def _():
        x = x_ref[...]
        ms = jnp.mean(x * x, axis=-1, keepdims=True)
        xn_ref[...] = ((x * lax.rsqrt(ms + EPS)) * g_ref[...]).astype(BF16)

    p = jnp.dot(xn_ref[...], w_ref[...], preferred_element_type=F32)

    lane = lax.broadcasted_iota(jnp.int32, (1, LANES), 1)
    first_half = (lane % HEAD_DIM) < (HEAD_DIM // 2)
    cos = cos_ref[...]
    sin = sin_ref[...]
    for c in range(PROJ_TN // LANES):
        sl = slice(c * LANES, (c + 1) * LANES)
        pc = p[:, sl]
        partner = jnp.where(first_half, pltpu.roll(pc, LANES - HEAD_DIM // 2, 1),
                            pltpu.roll(pc, HEAD_DIM // 2, 1))
        roped = pc * cos + partner * sin
        o_ref[c] = jnp.where(rope_ref[:, sl] > 0.0, roped, pc)


def _proj(x2d, g, w_bf, cos_t, sin_t, rope_flags, tm, pos_blocks):
    t = x2d.shape[0]
    grid = (t // tm, QKV_COLS // PROJ_TN)
    return pl.pallas_call(
        _proj_kernel,
        grid=grid,
        in_specs=[
            pl.BlockSpec((tm, D_MODEL), lambda i, j: (i, 0), pipeline_mode=pl.Buffered(1)),
            pl.BlockSpec((1, D_MODEL), lambda i, j: (0, 0)),
            pl.BlockSpec((D_MODEL, PROJ_TN), lambda i, j: (0, j)),
            pl.BlockSpec((tm, LANES), lambda i, j: (i % pos_blocks, 0)),
            pl.BlockSpec((tm, LANES), lambda i, j: (i % pos_blocks, 0)),
            pl.BlockSpec((1, PROJ_TN), lambda i, j: (0, j)),
        ],
        out_specs=pl.BlockSpec((PROJ_TN // LANES, tm, LANES), lambda i, j: (j, i, 0)),
        out_shape=jax.ShapeDtypeStruct((QKV_COLS // LANES, t, LANES), F32),
        scratch_shapes=[pltpu.VMEM((tm, D_MODEL), BF16)],
        compiler_params=_cparams(("arbitrary", "arbitrary")),
        name="proj",
    )(x2d, g, w_bf, cos_t, sin_t, rope_flags)


def _rope_tables(pos):
    half = HEAD_DIM // 2
    inv = ROPE_THETA ** (-jnp.arange(half, dtype=F32) / half)
    ang = pos.astype(F32)[:, None] * inv[None, :]
    cos = jnp.cos(ang)
    sin = jnp.sin(ang)
    cos_t = jnp.concatenate([cos, cos, cos, cos], axis=-1)
    sin_t = jnp.concatenate([-sin, sin, -sin, sin], axis=-1)
    return cos_t, sin_t


def _attn_a_kernel(sink_ref, q_ref, kp_ref, kc_ref, vp_ref, vc_ref, o_ref):
    kp_id = pl.program_id(1)
    n = pl.program_id(2)
    lane = lax.broadcasted_iota(jnp.int32, (1, LANES), 1)
    lo = lane < HEAD_DIM
    qi = lax.broadcasted_iota(jnp.int32, (BLOCK, 2 * BLOCK), 0)
    kj = lax.broadcasted_iota(jnp.int32, (BLOCK, 2 * BLOCK), 1)
    first_key = jnp.where(n == 0, BLOCK, 0)
    valid = (kj >= qi) & (kj <= qi + WINDOW_A) & (kj >= first_key)
    k2 = jnp.concatenate([kp_ref[...], kc_ref[...]], axis=0)
    v2 = jnp.concatenate([vp_ref[...], vc_ref[...]], axis=0)
    nt = (((1,), (1,)), ((), ()))
    for kvl in range(2):
        if kvl == 0:
            k_lo = jnp.where(lo, k2, 0.0)
            k_hi = pltpu.roll(k_lo, HEAD_DIM, 1)
            v_lo = jnp.where(lo, v2, 0.0)
            v_hi = pltpu.roll(v_lo, HEAD_DIM, 1)
        else:
            k_hi = jnp.where(lo, 0.0, k2)
            k_lo = pltpu.roll(k_hi, HEAD_DIM, 1)
            v_hi = jnp.where(lo, 0.0, v2)
            v_lo = pltpu.roll(v_hi, HEAD_DIM, 1)
        kb = (k_lo.astype(BF16), k_hi.astype(BF16))
        vb = (v_lo.astype(BF16), v_hi.astype(BF16))
        scores = []
        for c in range(G_A // 2):
            qc = q_ref[kvl * (G_A // 2) + c].astype(BF16)
            for par in range(2):
                scores.append(lax.dot_general(qc, kb[par], nt, preferred_element_type=F32) * SCALE)
        probs = []
        for i, s in enumerate(scores):
            sk = sink_ref[kp_id * (2 * G_A) + kvl * G_A + i]
            s = jnp.where(valid, s, NEG_INF)
            m = jnp.maximum(jnp.max(s, axis=-1, keepdims=True), sk)
            p = jnp.exp(s - m)
            den = jnp.sum(p, axis=-1, keepdims=True) + jnp.exp(sk - m)
            probs.append((p.astype(BF16), 1.0 / den))
        for c in range(G_A // 2):
            col = (kvl * (G_A // 2) + c) * LANES
            acc = jnp.zeros((BLOCK, LANES), F32)
            for par in range(2):
                p, rden = probs[2 * c + par]
                acc = acc + jnp.dot(p, vb[par], preferred_element_type=F32) * rden
            o_ref[:, col:col + LANES] = acc


def _attn_a(p4, sink):
    _, b, s, _ = p4.shape
    nb = s // BLOCK
    qb = 2 * G_A * HEAD_DIM // LANES
    grid = (b, N_KV_A // 2, nb)
    prev = lambda n: jnp.maximum(n - 1, 0)
    return pl.pallas_call(
        _attn_a_kernel,
        grid=grid,
        in_specs=[
            pl.BlockSpec(memory_space=pltpu.SMEM),
            pl.BlockSpec((qb, None, BLOCK, LANES), lambda bi, kp, n: (kp, bi, n, 0)),
            pl.BlockSpec((None, None, BLOCK, LANES), lambda bi, kp, n: (COL_KA + kp, bi, prev(n), 0)),
            pl.BlockSpec((None, None, BLOCK, LANES), lambda bi, kp, n: (COL_KA + kp, bi, n, 0)),
            pl.BlockSpec((None, None, BLOCK, LANES), lambda bi, kp, n: (COL_VA + kp, bi, prev(n), 0)),
            pl.BlockSpec((None, None, BLOCK, LANES), lambda bi, kp, n: (COL_VA + kp, bi, n, 0)),
        ],
        out_specs=pl.BlockSpec((None, BLOCK, qb * LANES), lambda bi, kp, n: (bi, n, kp)),
        out_shape=jax.ShapeDtypeStruct((b, s, WIDTH_A), F32),
        compiler_params=_cparams(("arbitrary", "arbitrary", "arbitrary")),
        name="attn_a",
    )(sink, p4, p4, p4, p4, p4)


def _attn_b_kernel(q_ref, k_ref, v_ref, o_ref, oc_ref, lc_ref):
    seq = q_ref.shape[0]
    lane = lax.broadcasted_iota(jnp.int32, (1, LANES), 1)
    lo = lane < HEAD_DIM
    nt = (((1,), (1,)), ((), ()))

    masks = (lo, jnp.logical_not(lo))

    def blocks(cfg, d, starts, has_prev):
        nk = 2 * BLOCK if has_prev else BLOCK
        qi = lax.broadcasted_iota(jnp.int32, (BLOCK, nk), 0)
        kj = lax.broadcasted_iota(jnp.int32, (BLOCK, nk), 1)
        if has_prev:
            valid = (kj >= qi) & (kj <= qi + BLOCK)
        else:
            valid = kj <= qi
        rows_qs, vs, scores = [], [], []
        for start in starts:
            rows_q = pl.ds(start, BLOCK, stride=d) if d > 1 else pl.ds(start, BLOCK)
            kstart = start - BLOCK * d if has_prev else start
            rows_k = pl.ds(kstart, nk, stride=d) if d > 1 else pl.ds(kstart, nk)
            q = q_ref[rows_q, :].astype(BF16)
            k = k_ref[rows_k, :]
            rows_qs.append(rows_q)
            vs.append(v_ref[rows_k, :])
            for msk in masks:
                kpar = jnp.where(msk, k, 0.0).astype(BF16)
                scores.append(lax.dot_general(q, kpar, nt, preferred_element_type=F32) * SCALE)
        probs = []
        for s in scores:
            s = jnp.where(valid, s, NEG_INF)
            m = jnp.max(s, axis=-1, keepdims=True)
            p = jnp.exp(s - m)
            den = jnp.sum(p, axis=-1, keepdims=True)
            probs.append((p.astype(BF16), 1.0 / den, jnp.log(den) + m))
        for b, (rows_q, v) in enumerate(zip(rows_qs, vs)):
            acc = jnp.zeros((BLOCK, LANES), F32)
            lse = jnp.zeros((BLOCK, LANES), F32)
            for par, msk in enumerate(masks):
                p, rden, l = probs[2 * b + par]
                vpar = jnp.where(msk, v, 0.0).astype(BF16)
                acc = acc + jnp.dot(p, vpar, preferred_element_type=F32) * rden
                lse = jnp.where(msk, l, lse)
            oc_ref[cfg, rows_q, :] = acc
            lc_ref[cfg, rows_q, :] = lse

    def block(cfg, d, start, has_prev):
        blocks(cfg, d, [start], has_prev)

    for cfg, (w, d) in enumerate(DILATED):
        assert w // d == BLOCK
        nblk = seq // d // BLOCK
        span = BLOCK * d
        if d == 1:
            block(cfg, d, 0, False)
            cnt = nblk - 1
            group = next(gs for gs in (5, 4, 3, 2, 1) if cnt % gs == 0)

            def body_seq(it, carry, cfg=cfg, d=d, span=span, group=group):
                blocks(cfg, d, [(1 + it * group + u) * span for u in range(group)], True)
                return carry

            lax.fori_loop(0, cnt // group, body_seq, 0)
            continue
        cg = min(d, 4)
        ncg = d // cg

        def body_first(it, carry, cfg=cfg, d=d, cg=cg):
            blocks(cfg, d, [it * cg + u for u in range(cg)], False)
            return carry

        if ncg == 1:
            body_first(0, 0)
        else:
            lax.fori_loop(0, ncg, body_first, 0)
        if nblk > 1:
            def body_rest(it, carry, cfg=cfg, d=d, cg=cg, ncg=ncg, span=span):
                n = 1 + it // ncg
                rc = it % ncg
                blocks(cfg, d, [rc * cg + u + n * span for u in range(cg)], True)
                return carry

            lax.fori_loop(0, (nblk - 1) * ncg, body_rest, 0)

    def combine(i, carry):
        rows = pl.ds(pl.multiple_of(i * BLOCK, BLOCK), BLOCK)
        l0 = lc_ref[0, rows, :]
        l1 = lc_ref[1, rows, :]
        l2 = lc_ref[2, rows, :]
        mx = jnp.maximum(jnp.maximum(l0, l1), l2)
        w0 = jnp.exp(l0 - mx)
        w1 = jnp.exp(l1 - mx)
        w2 = jnp.exp(l2 - mx)
        num = w0 * oc_ref[0, rows, :] + w1 * oc_ref[1, rows, :] + w2 * oc_ref[2, rows, :]
        o_ref[rows, :] = num / (w0 + w1 + w2)
        return carry

    lax.fori_loop(0, seq // BLOCK, combine, 0)


def _attn_b(p4):
    _, b, s, _ = p4.shape
    grid = (b, N_HEADS_B // 2)
    ncfg = len(DILATED)
    return pl.pallas_call(
        _attn_b_kernel,
        grid=grid,
        in_specs=[
            pl.BlockSpec((None, None, s, LANES), lambda bi, hp: (COL_QB + hp, bi, 0, 0)),
            pl.BlockSpec((None, None, s, LANES), lambda bi, hp: (COL_KB + hp, bi, 0, 0)),
            pl.BlockSpec((None, None, s, LANES), lambda bi, hp: (COL_VB + hp, bi, 0, 0)),
        ],
        out_specs=pl.BlockSpec((None, s, LANES), lambda bi, hp: (bi, 0, hp)),
        out_shape=jax.ShapeDtypeStruct((b, s, WIDTH_B), F32),
        scratch_shapes=[pltpu.VMEM((ncfg, s, LANES), F32), pltpu.VMEM((ncfg, s, LANES), F32)],
        compiler_params=_cparams(("arbitrary", "arbitrary")),
        name="attn_b",
    )(p4, p4, p4)


SAMP_A_BB = 8


def _samp_a_kernel(q_ref, kt_ref, vt_ref, kn_ref, vn_ref, sink_ref, o_ref):
    nt = (((1,), (1,)), ((), ()))
    for bb in range(SAMP_A_BB):
        for kv in range(N_KV_A):
            q = q_ref[bb, kv * G_A:(kv + 1) * G_A, :]
            kt = kt_ref[bb, kv]
            vt = vt_ref[bb, kv]
            kn = kn_ref[bb, kv:kv + 1, :]
            vn = vn_ref[bb, kv:kv + 1, :]
            sk = sink_ref[kv]
            s = jnp.dot(q.astype(BF16), kt.astype(BF16), preferred_element_type=F32) * SCALE
            sn = jnp.sum(q * kn, axis=-1, keepdims=True) * SCALE
            m = jnp.maximum(jnp.maximum(jnp.max(s, axis=-1, keepdims=True), sn), sk)
            p = jnp.exp(s - m)
            pn = jnp.exp(sn - m)
            den = jnp.sum(p, axis=-1, keepdims=True) + pn + jnp.exp(sk - m)
            o = lax.dot_general(p.astype(BF16), vt.astype(BF16), nt, preferred_element_type=F32)
            o_ref[bb, kv * G_A:(kv + 1) * G_A, :] = (o + pn * vn) / den


def _samp_a(qa_s, kt_a, vt_a, kn, vn, sink3):
    db = qa_s.shape[0]
    lb = kt_a.shape[-1]
    assert lb <= WINDOW_A
    bb = SAMP_A_BB
    return pl.pallas_call(
        _samp_a_kernel,
        grid=(db // bb,),
        in_specs=[
            pl.BlockSpec((bb, N_HEADS_A, HEAD_DIM), lambda b: (b, 0, 0)),
            pl.BlockSpec((bb, N_KV_A, HEAD_DIM, lb), lambda b: (b, 0, 0, 0)),
            pl.BlockSpec((bb, N_KV_A, HEAD_DIM, lb), lambda b: (b, 0, 0, 0)),
            pl.BlockSpec((bb, N_KV_A, HEAD_DIM), lambda b: (b, 0, 0)),
            pl.BlockSpec((bb, N_KV_A, HEAD_DIM), lambda b: (b, 0, 0)),
            pl.BlockSpec((N_KV_A, G_A, 1), lambda b: (0, 0, 0)),
        ],
        out_specs=pl.BlockSpec((bb, N_HEADS_A, HEAD_DIM), lambda b: (b, 0, 0)),
        out_shape=jax.ShapeDtypeStruct((db, N_HEADS_A, HEAD_DIM), F32),
        compiler_params=_cparams(("arbitrary",)),
        name="samp_a",
    )(qa_s, kt_a, vt_a, kn, vn, sink3)


SAMP_B_HG = 8


def _samp_b_kernel(kt_ref, vt_ref, qt_ref, knt_ref, vnt_ref, o_ref):
    b = pl.program_id(1)
    lb = kt_ref.shape[-1]
    laneb = lax.broadcasted_iota(jnp.int32, (1, LANES), 1) == b

    def column(ref):
        return jnp.sum(jnp.where(laneb, ref[...], 0.0), axis=-1, keepdims=True)

    qcol = column(qt_ref)
    kncol = column(knt_ref)
    vncol = column(vnt_ref)
    hrow = lax.broadcasted_iota(jnp.int32, (SAMP_B_HG, 1), 0)
    s = jnp.zeros((SAMP_B_HG, lb), F32)
    sn = jnp.zeros((SAMP_B_HG, 1), F32)
    for h in range(SAMP_B_HG):
        hs = slice(h * HEAD_DIM, (h + 1) * HEAD_DIM)
        s_h = jnp.sum(kt_ref[0, h] * qcol[hs], axis=0, keepdims=True)
        sn_h = jnp.sum(qcol[hs] * kncol[hs], axis=0, keepdims=True)
        s = jnp.where(hrow == h, s_h, s)
        sn = jnp.where(hrow == h, sn_h, sn)
    s = s * SCALE
    sn = sn * SCALE
    dist = lb - lax.broadcasted_iota(jnp.int32, (1, lb), 1)
    ps, pns, lses = [], [], []
    for w, d in DILATED:
        valid = ((dist % d) == 0) & (dist <= w)
        m = jnp.maximum(jnp.max(jnp.where(valid, s, NEG_INF), axis=-1, keepdims=True), sn)
        p = jnp.where(valid, jnp.exp(s - m), 0.0)
        pn = jnp.exp(sn - m)
        den = jnp.sum(p, axis=-1, keepdims=True) + pn
        ps.append(p / den)
        pns.append(pn / den)
        lses.append(jnp.log(den) + m)
    mx = functools.reduce(jnp.maximum, lses)
    ws = [jnp.exp(l - mx) for l in lses]
    wsum = functools.reduce(lambda a, c: a + c, ws)
    pmix = functools.reduce(lambda a, c: a + c, [w_ * p_ for w_, p_ in zip(ws, ps)]) / wsum
    pnmix = functools.reduce(lambda a, c: a + c, [w_ * p_ for w_, p_ in zip(ws, pns)]) / wsum
    cols = []
    for h in range(SAMP_B_HG):
        hs = slice(h * HEAD_DIM, (h + 1) * HEAD_DIM)
        oc = jnp.sum(vt_ref[0, h] * pmix[h:h + 1, :], axis=-1, keepdims=True)
        cols.append(oc + pnmix[h:h + 1, :] * vncol[hs])
    ocol = jnp.concatenate(cols, axis=0)

    @pl.when(b == 0)
    def _():
        o_ref[...] = jnp.zeros_like(o_ref)

    o_ref[...] = jnp.where(laneb, ocol, o_ref[...])


def _samp_b(kt_b, vt_b, qt, knt, vnt):
    db, nh, hd, lb = kt_b.shape
    assert db == LANES
    rows = SAMP_B_HG * HEAD_DIM
    grid = (nh // SAMP_B_HG, db)
    return pl.pallas_call(
        _samp_b_kernel,
        grid=grid,
        in_specs=[
            pl.BlockSpec((1, SAMP_B_HG, hd, lb), lambda g, b: (b, g, 0, 0)),
            pl.BlockSpec((1, SAMP_B_HG, hd, lb), lambda g, b: (b, g, 0, 0)),
            pl.BlockSpec((rows, db), lambda g, b: (g, 0)),
            pl.BlockSpec((rows, db), lambda g, b: (g, 0)),
            pl.BlockSpec((rows, db), lambda g, b: (g, 0)),
        ],
        out_specs=pl.BlockSpec((rows, db), lambda g, b: (g, 0)),
        out_shape=jax.ShapeDtypeStruct((nh * hd, db), F32),
        compiler_params=_cparams(("arbitrary", "arbitrary")),
        name="samp_b",
    )(kt_b, vt_b, qt, knt, vnt)


MERGE_TN = 1024
MERGE_NORM_ROWS = 512


def _merge_kernel(oa_ref, ob_ref, x_ref, ga_ref, gb_ref, w_ref, gfc_ref, ht_ref, xt_ref, cat_ref, ssq_ref):
    j = pl.program_id(1)
    nj = pl.num_programs(1)

    @pl.when(j == 0)
    def _():
        oa = oa_ref[...]
        ob = ob_ref[...]
        ya = (oa * lax.rsqrt(jnp.mean(oa * oa, axis=-1, keepdims=True) + EPS)) * ga_ref[...]
        yb = (ob * lax.rsqrt(jnp.mean(ob * ob, axis=-1, keepdims=True) + EPS)) * gb_ref[...]
        cat_ref[:, :WIDTH_A] = ya.astype(BF16)
        cat_ref[:, WIDTH_A:] = yb.astype(BF16)
        ssq_ref[...] = jnp.zeros_like(ssq_ref)

    h_t = (x_ref[...] + jnp.dot(cat_ref[...], w_ref[...], preferred_element_type=F32)).T
    ht_ref[pl.ds(pl.multiple_of(j * MERGE_TN, MERGE_TN), MERGE_TN), :] = h_t
    ssq_ref[...] += jnp.sum(h_t * h_t, axis=0, keepdims=True)

    @pl.when(j == nj - 1)
    def _():
        rinv = lax.rsqrt(ssq_ref[...] * (1.0 / D_MODEL) + EPS)
        for c in range(D_MODEL // MERGE_NORM_ROWS):
            rs = slice(c * MERGE_NORM_ROWS, (c + 1) * MERGE_NORM_ROWS)
            xt_ref[rs, :] = ((ht_ref[rs, :] * rinv) * gfc_ref[rs, :]).astype(BF16)


def _merge(oa, ob, x2d, ga, gb, w_bf, gf_col, tm):
    t = x2d.shape[0]
    nj = D_MODEL // MERGE_TN
    grid = (t // tm, nj)
    once = pl.Buffered(1)
    return pl.pallas_call(
        _merge_kernel,
        grid=grid,
        in_specs=[
            pl.BlockSpec((tm, WIDTH_A), lambda i, j: (i, 0), pipeline_mode=once),
            pl.BlockSpec((tm, WIDTH_B), lambda i, j: (i, 0), pipeline_mode=once),
            pl.BlockSpec((tm, MERGE_TN), lambda i, j: (i, j)),
            pl.BlockSpec((1, WIDTH_A), lambda i, j: (0, 0)),
            pl.BlockSpec((1, WIDTH_B), lambda i, j: (0, 0)),
            pl.BlockSpec((D_MODEL, MERGE_TN), lambda i, j: (0, j)),
            pl.BlockSpec((D_MODEL, 1), lambda i, j: (0, 0), pipeline_mode=once),
        ],
        out_specs=[
            pl.BlockSpec((D_MODEL, tm), lambda i, j: (0, i), pipeline_mode=once),
            pl.BlockSpec((D_MODEL, tm), lambda i, j: (0, i), pipeline_mode=once),
        ],
        out_shape=[
            jax.ShapeDtypeStruct((D_MODEL, t), F32),
            jax.ShapeDtypeStruct((D_MODEL, t), BF16),
        ],
        scratch_shapes=[pltpu.VMEM((tm, D_MODEL), BF16), pltpu.VMEM((1, tm), F32)],
        compiler_params=_cparams(("arbitrary", "arbitrary")),
        name="merge",
    )(oa, ob, x2d, ga, gb, w_bf, gf_col)


GATE_ROWS = 4


def _top_values(x, k):
    row = lax.broadcasted_iota(jnp.int32, (k, x.shape[1]), 0)
    vals = jnp.zeros((k, x.shape[1]), F32)
    cur = x
    for r in range(k):
        m = jnp.max(cur, axis=0, keepdims=True)
        vals = jnp.where(row == r, m, vals)
        if r + 1 < k:
            cur = jnp.where(cur == m, NEG_INF, cur)
    return vals


def _router_kernel(x_ref, wq_ref, k1_ref, k2_ref, th_ref, e1_ref, s2_ref, e2_ref):
    half = PEER_DKEY // 2
    qt = jnp.dot(wq_ref[...], x_ref[...], preferred_element_type=F32)
    s1_all = jnp.dot(k1_ref[0], qt[:half].astype(BF16), preferred_element_type=F32)
    s2_all = jnp.dot(k2_ref[0], qt[half:].astype(BF16), preferred_element_type=F32)
    for c in range(x_ref.shape[1] // LANES):
        sl = slice(c * LANES, (c + 1) * LANES)
        s1 = s1_all[:, sl]
        s2 = s2_all[:, sl]
        v1 = _top_values(s1, PEER_TOPK)
        v2 = _top_values(s2, PEER_TOPK)
        cands = [v1[0:1] + v2[0:8], v1[0:1] + v2[8:16]]
        cands += [v1[a:a + 1] + v2[0:8] for a in range(1, 8)]
        cands += [v1[8:16] + v2[0:1]]
        cand = jnp.concatenate(cands, axis=0)
        tk = _top_values(cand, PEER_TOPK)[PEER_TOPK - 1:PEER_TOPK]
        m1 = v1[0:1]
        m2 = v2[0:1]
        z = jnp.sum(jnp.where(cand >= tk, jnp.exp(cand - (m1 + m2)), 0.0), axis=0, keepdims=True)
        theta = jnp.full(s1.shape, jnp.inf, F32)
        for b in range(PEER_TOPK):
            vb = v2[b:b + 1]
            theta = jnp.where((s1 + vb) >= tk, vb, theta)
        theta = jnp.where(s1 >= v1[PEER_TOPK - 1:PEER_TOPK], theta, jnp.inf)
        e1 = jnp.exp(s1 - m1) / z
        for grp in range(PEER_NKEYS // GATE_ROWS):
            rows = slice(grp * GATE_ROWS, (grp + 1) * GATE_ROWS)
            th_ref[0, grp, :, sl] = theta[rows]
            e1_ref[0, grp, :, sl] = e1[rows]
        s2_ref[0, :, sl] = s2
        e2_ref[0, :, sl] = jnp.exp(s2 - m2)


def _router(xt, wqt, k1b, k2b, tm):
    t = xt.shape[1]
    grid = (t // tm, PEER_HEADS)
    out = jax.ShapeDtypeStruct((PEER_HEADS, PEER_NKEYS, t), F32)
    ospec = pl.BlockSpec((1, PEER_NKEYS, tm), lambda i, h: (h, 0, i))
    ngrp = PEER_NKEYS // GATE_ROWS
    gout = jax.ShapeDtypeStruct((PEER_HEADS, ngrp, GATE_ROWS, t), F32)
    gspec = pl.BlockSpec((1, ngrp, GATE_ROWS, tm), lambda i, h: (h, 0, 0, i))
    return pl.pallas_call(
        _router_kernel,
        grid=grid,
        in_specs=[
            pl.BlockSpec((D_MODEL, tm), lambda i, h: (0, i)),
            pl.BlockSpec((PEER_DKEY, D_MODEL), lambda i, h: (h, 0)),
            pl.BlockSpec((1, PEER_NKEYS, PEER_DKEY // 2), lambda i, h: (h, 0, 0)),
            pl.BlockSpec((1, PEER_NKEYS, PEER_DKEY // 2), lambda i, h: (h, 0, 0)),
        ],
        out_specs=[gspec, gspec, ospec, ospec],
        out_shape=[gout, gout, out, out],
        compiler_params=_cparams(("arbitrary", "arbitrary")),
        name="router",
    )(xt, wqt, k1b, k2b)


EXP_EB = 512
EXP_IB = EXP_EB // PEER_NKEYS
assert EXP_IB == GATE_ROWS


def _experts_kernel(x_ref, u_ref, vt_ref, th_ref, e1_ref, s2_ref, e2_ref, o_ref, at_ref):
    e = pl.program_id(1)

    @pl.when(e == 0)
    def _():
        o_ref[...] = jnp.zeros_like(o_ref)

    ht = jnp.dot(u_ref[...], x_ref[...], preferred_element_type=F32)
    for il in range(EXP_IB):
        rs = slice(il * PEER_NKEYS, (il + 1) * PEER_NKEYS)
        for c in range(x_ref.shape[1] // LANES):
            sl = slice(c * LANES, (c + 1) * LANES)
            g = jnp.zeros((PEER_NKEYS, LANES), F32)
            for h in range(PEER_HEADS):
                th = th_ref[h, 0, il:il + 1, sl]
                e1 = e1_ref[h, 0, il:il + 1, sl]
                g = g + jnp.where(s2_ref[h, :, sl] >= th, e2_ref[h, :, sl], 0.0) * e1
            at_ref[rs, sl] = (jax.nn.gelu(ht[rs, sl]) * g).astype(BF16)
    o_ref[...] += jnp.dot(vt_ref[...], at_ref[...], preferred_element_type=F32)


def _experts(xt, u_bf, vt_bf, th4, e14, s2, e2, tm):
    t = xt.shape[1]
    grid = (t // tm, PEER_EXPERTS // EXP_EB)
    return pl.pallas_call(
        _experts_kernel,
        grid=grid,
        in_specs=[
            pl.BlockSpec((D_MODEL, tm), lambda i, e: (0, i)),
            pl.BlockSpec((EXP_EB, D_MODEL), lambda i, e: (e, 0)),
            pl.BlockSpec((D_MODEL, EXP_EB), lambda i, e: (0, e)),
            pl.BlockSpec((PEER_HEADS, 1, EXP_IB, tm), lambda i, e: (0, e, 0, i)),
            pl.BlockSpec((PEER_HEADS, 1, EXP_IB, tm), lambda i, e: (0, e, 0, i)),
            pl.BlockSpec((PEER_HEADS, PEER_NKEYS, tm), lambda i, e: (0, 0, i)),
            pl.BlockSpec((PEER_HEADS, PEER_NKEYS, tm), lambda i, e: (0, 0, i)),
        ],
        out_specs=pl.BlockSpec((D_MODEL, tm), lambda i, e: (0, i)),
        out_shape=jax.ShapeDtypeStruct((D_MODEL, t), F32),
        scratch_shapes=[pltpu.VMEM((EXP_EB, tm), BF16)],
        compiler_params=_cparams(("arbitrary", "arbitrary")),
        name="experts",
    )(xt, u_bf, vt_bf, th4, e14, s2, e2)


CAST_ROWS = 1024
CAST_COLS = 2048


def _cast_t_kernel(v_ref, o_ref):
    o_ref[...] = v_ref[...].T.astype(BF16)


def _cast_transposed(v):
    e, d = v.shape
    return pl.pallas_call(
        _cast_t_kernel,
        grid=(e // CAST_ROWS, d // CAST_COLS),
        in_specs=[pl.BlockSpec((CAST_ROWS, CAST_COLS), lambda i, j: (i, j))],
        out_specs=pl.BlockSpec((CAST_COLS, CAST_ROWS), lambda i, j: (j, i)),
        out_shape=jax.ShapeDtypeStruct((d, e), BF16),
        compiler_params=_cparams(("arbitrary", "arbitrary")),
        name="cast_t",
    )(v)


def _final_kernel(ht_ref, ft_ref, g_ref, y_ref):
    h = ht_ref[...] + ft_ref[...]
    ms = jnp.mean(h * h, axis=0, keepdims=True)
    y = (h * lax.rsqrt(ms + EPS)) * g_ref[...]
    y_ref[...] = y.T


def _final(ht, ft, g_col, tm):
    t = ht.shape[1]
    return pl.pallas_call(
        _final_kernel,
        grid=(t // tm,),
        in_specs=[
            pl.BlockSpec((D_MODEL, tm), lambda i: (0, i)),
            pl.BlockSpec((D_MODEL, tm), lambda i: (0, i)),
            pl.BlockSpec((D_MODEL, 1), lambda i: (0, 0)),
        ],
        out_specs=pl.BlockSpec((tm, D_MODEL), lambda i: (i, 0)),
        out_shape=jax.ShapeDtypeStruct((t, D_MODEL), F32),
        compiler_params=_cparams(("arbitrary",)),
        name="final",
    )(ht, ft, g_col)


def kernel(x_prompt, x_sample, state_a_k, state_a_v, state_b_k, state_b_v, g_attn, w_in, attn_sink,
           g_out_a, g_out_b, w_out, g_ffn, peer_wq, peer_k1, peer_k2, peer_u, peer_v, g_final):
    bsz, seq, _ = x_prompt.shape
    db, dt, _ = x_sample.shape
    assert w_in.shape[0] == 1 and dt == 1
    past = PAST_LEN
    lb = state_b_k.shape[2]
    assert lb == max(w for w, _ in DILATED)

    w_in_b = w_in[0].astype(BF16)
    w_out_b = w_out[0].astype(BF16)
    wq_t = peer_wq[0].reshape(D_MODEL, PEER_HEADS * PEER_DKEY).T.astype(BF16)
    k1_b = peer_k1[0].astype(BF16)
    k2_b = peer_k2[0].astype(BF16)
    u_b = peer_u[0].astype(BF16)
    vt_b = _cast_transposed(peer_v[0])

    rope_flags = jnp.concatenate([
        jnp.ones((1, WIDTH_A + N_KV_A * HEAD_DIM), F32), jnp.zeros((1, N_KV_A * HEAD_DIM), F32),
        jnp.ones((1, 2 * WIDTH_B), F32), jnp.zeros((1, WIDTH_B), F32)], axis=-1)
    cos_p, sin_p = _rope_tables(jnp.arange(seq))
    cos_s, sin_s = _rope_tables(jnp.full((db,), past))

    tm_proj, tm_merge = 1024, 512
    gf_col = g_ffn.reshape(D_MODEL, 1)
    xp = x_prompt.reshape(bsz * seq, D_MODEL)
    pp = _proj(xp, g_attn, w_in_b, cos_p, sin_p, rope_flags, tm_proj, seq // tm_proj)
    pp4 = pp.reshape(QKV_COLS // LANES, bsz, seq, LANES)
    oa_p = _attn_a(pp4, attn_sink[0])
    ob_p = _attn_b(pp4)
    ht_p, xt_p = _merge(oa_p.reshape(bsz * seq, WIDTH_A), ob_p.reshape(bsz * seq, WIDTH_B), xp,
                        g_out_a, g_out_b, w_out_b, gf_col, tm_merge)

    xs = x_sample.reshape(db, D_MODEL)
    ps = _proj(xs, g_attn, w_in_b, cos_s, sin_s, rope_flags, db, 1)

    def sample_cols(lo, hi):
        return jnp.transpose(ps[lo:hi], (1, 0, 2)).reshape(db, (hi - lo) * LANES)

    qa_s = sample_cols(COL_QA, COL_KA).reshape(db, N_HEADS_A, HEAD_DIM)
    ka_s = sample_cols(COL_KA, COL_VA).reshape(db, N_KV_A, HEAD_DIM)
    va_s = sample_cols(COL_VA, COL_QB).reshape(db, N_KV_A, HEAD_DIM)
    kb_s = sample_cols(COL_KB, COL_VB)
    vb_s = sample_cols(COL_VB, QKV_COLS // LANES)
    kt_a = jnp.transpose(state_a_k[0], (0, 2, 3, 1))
    vt_a = jnp.transpose(state_a_v[0], (0, 2, 3, 1))
    skt_b = jnp.transpose(state_b_k[0], (0, 2, 3, 1))
    svt_b = jnp.transpose(state_b_v[0], (0, 2, 3, 1))
    oa_s = _samp_a(qa_s, kt_a, vt_a, ka_s, va_s, attn_sink[0].reshape(N_KV_A, G_A, 1))
    obt_s = _samp_b(skt_b, svt_b, sample_cols(COL_QB, COL_KB).T, kb_s.T, vb_s.T)
    ht_s, xt_s = _merge(oa_s.reshape(db, WIDTH_A), obt_s.T, xs, g_out_a, g_out_b, w_out_b, gf_col, db)

    g_col = g_final.reshape(D_MODEL, 1)

    def peer_and_final(ht, xt, tm_peer, tm_final):
        th, e1, s2, e2 = _router(xt, wq_t, k1_b, k2_b, tm_peer)
        ft = _experts(xt, u_b, vt_b, th, e1, s2, e2, tm_peer)
        return _final(ht, ft, g_col, tm_final)

    y_p = peer_and_final(ht_p, xt_p, 512, 256).reshape(bsz, seq, D_MODEL)
    y_s = peer_and_final(ht_s, xt_s, db, db).reshape(db, 1, D_MODEL)

    def prompt_cols(lo, hi, first_row, heads):
        blk = jnp.transpose(pp4[lo:hi, :, first_row:], (1, 2, 0, 3))
        return blk.reshape(1, bsz, seq - first_row, heads, HEAD_DIM)

    rows_a = min(WINDOW_A, seq)
    ka_p = prompt_cols(COL_KA, COL_VA, seq - rows_a, N_KV_A)
    va_p = prompt_cols(COL_VA, COL_QB, seq - rows_a, N_KV_A)
    kb_p = prompt_cols(COL_KB, COL_VB, 0, N_HEADS_B)
    vb_p = prompt_cols(COL_VB, QKV_COLS // LANES, 0, N_HEADS_B)
    return (y_p, y_s, ka_p, va_p, kb_p, vb_p,
            ka_s.reshape(1, db, 1, N_KV_A, HEAD_DIM), va_s.reshape(1, db, 1, N_KV_A, HEAD_DIM),
            kb_s.reshape(1, db, 1, N_HEADS_B, HEAD_DIM), vb_s.reshape(1, db, 1, N_HEADS_B, HEAD_DIM))
```

```python
import functools
import math

import jax
import jax.numpy as jnp
import numpy as np
from jax import lax
from jax.experimental import pallas as pl
from jax.experimental.pallas import tpu as pltpu

D_MODEL = 4096
HEAD_DIM = 64
WIDTH_A = 2048
WIDTH_B = 2048
N_HEADS_A = 32
N_KV_A = 4
G_A = 8
N_HEADS_B = 32
WINDOW_A = 128
DILATED = ((128, 1), (512, 4), (2048, 16))
BLOCK = 128
PAST_LEN = 8192
ROPE_THETA = 10000.0
SCALE = HEAD_DIM ** -0.5
PEER_HEADS = 8
PEER_NKEYS = 128
PEER_EXPERTS = PEER_NKEYS * PEER_NKEYS
PEER_DKEY = 256
PEER_TOPK = 16
EPS = 1e-6
QKV_COLS = WIDTH_A + 2 * N_KV_A * HEAD_DIM + 3 * WIDTH_B

COL_QA = 0
COL_KA = WIDTH_A // 128
COL_VA = COL_KA + N_KV_A * HEAD_DIM // 128
COL_QB = COL_VA + N_KV_A * HEAD_DIM // 128
COL_KB = COL_QB + WIDTH_B // 128
COL_VB = COL_KB + WIDTH_B // 128

LANES = 128
VMEM_LIMIT = 60 * 1024 * 1024

BF16 = jnp.bfloat16
F32 = jnp.float32
NEG_INF = float("-inf")


def _cparams(sem):
    return pltpu.CompilerParams(dimension_semantics=sem, vmem_limit_bytes=VMEM_LIMIT)


PROJ_TN = 512


def _proj_kernel(x_ref, g_ref, w_ref, cos_ref, sin_ref, rope_ref, o_ref, xn_ref):
    j = pl.program_id(1)

    @pl.when(j == 0)
    def _():
        x = x_ref[...]
        ms = jnp.mean(x * x, axis=-1, keepdims=True)
        xn_ref[...] = ((x * lax.rsqrt(ms + EPS)) * g_ref[...]).astype(BF16)

    p = jnp.dot(xn_ref[...], w_ref[...], preferred_element_type=F32)

    lane = lax.broadcasted_iota(jnp.int32, (1, LANES), 1)
    first_half = (lane % HEAD_DIM) < (HEAD_DIM // 2)
    cos = cos_ref[...]
    sin = sin_ref[...]
    for c in range(PROJ_TN // LANES):
        sl = slice(c * LANES, (c + 1) * LANES)
        pc = p[:, sl]
        partner = jnp.where(first_half, pltpu.roll(pc, LANES - HEAD_DIM // 2, 1),
                            pltpu.roll(pc, HEAD_DIM // 2, 1))
        roped = pc * cos + partner * sin
        o_ref[c] = jnp.where(rope_ref[:, sl] > 0.0, roped, pc)


def _proj(x2d, g, w_bf, cos_t, sin_t, rope_flags, tm, pos_blocks):
    t = x2d.shape[0]
    grid = (t // tm, QKV_COLS // PROJ_TN)
    return pl.pallas_call(
        _proj_kernel,
        grid=grid,
        in_specs=[
            pl.BlockSpec((tm, D_MODEL), lambda i, j: (i, 0), pipeline_mode=pl.Buffered(1)),
            pl.BlockSpec((1, D_MODEL), lambda i, j: (0, 0)),
            pl.BlockSpec((None, D_MODEL, PROJ_TN), lambda i, j: (j, 0, 0)),
            pl.BlockSpec((tm, LANES), lambda i, j: (i % pos_blocks, 0)),
            pl.BlockSpec((tm, LANES), lambda i, j: (i % pos_blocks, 0)),
            pl.BlockSpec((1, PROJ_TN), lambda i, j: (0, j)),
        ],
        out_specs=pl.BlockSpec((PROJ_TN // LANES, tm, LANES), lambda i, j: (j, i, 0)),
        out_shape=jax.ShapeDtypeStruct((QKV_COLS // LANES, t, LANES), F32),
        scratch_shapes=[pltpu.VMEM((tm, D_MODEL), BF16)],
        compiler_params=_cparams(("arbitrary", "arbitrary")),
        name="proj",
    )(x2d, g, w_bf, cos_t, sin_t, rope_flags)


def _rope_tables(pos):
    half = HEAD_DIM // 2
    inv = ROPE_THETA ** (-jnp.arange(half, dtype=F32) / half)
    ang = pos.astype(F32)[:, None] * inv[None, :]
    cos = jnp.cos(ang)
    sin = jnp.sin(ang)
    cos_t = jnp.concatenate([cos, cos, cos, cos], axis=-1)
    sin_t = jnp.concatenate([-sin, sin, -sin, sin], axis=-1)
    return cos_t, sin_t


def _attn_a_kernel(sink_ref, q_ref, kp_ref, kc_ref, vp_ref, vc_ref, o_ref):
    kp_id = pl.program_id(1)
    n = pl.program_id(2)
    lane = lax.broadcasted_iota(jnp.int32, (1, LANES), 1)
    lo = lane < HEAD_DIM
    qi = lax.broadcasted_iota(jnp.int32, (BLOCK, 2 * BLOCK), 0)
    kj = lax.broadcasted_iota(jnp.int32, (BLOCK, 2 * BLOCK), 1)
    first_key = jnp.where(n == 0, BLOCK, 0)
    valid = (kj >= qi) & (kj <= qi + WINDOW_A) & (kj >= first_key)
    k2 = jnp.concatenate([kp_ref[...], kc_ref[...]], axis=0)
    v2 = jnp.concatenate([vp_ref[...], vc_ref[...]], axis=0)
    nt = (((1,), (1,)), ((), ()))
    for kvl in range(2):
        if kvl == 0:
            k_lo = jnp.where(lo, k2, 0.0)
            k_hi = pltpu.roll(k_lo, HEAD_DIM, 1)
            v_lo = jnp.where(lo, v2, 0.0)
            v_hi = pltpu.roll(v_lo, HEAD_DIM, 1)
        else:
            k_hi = jnp.where(lo, 0.0, k2)
            k_lo = pltpu.roll(k_hi, HEAD_DIM, 1)
            v_hi = jnp.where(lo, 0.0, v2)
            v_lo = pltpu.roll(v_hi, HEAD_DIM, 1)
        kb = (k_lo.astype(BF16), k_hi.astype(BF16))
        vb = (v_lo.astype(BF16), v_hi.astype(BF16))
        scores = []
        for c in range(G_A // 2):
            qc = q_ref[kvl * (G_A // 2) + c].astype(BF16)
            for par in range(2):
                scores.append(lax.dot_general(qc, kb[par], nt, preferred_element_type=F32) * SCALE)
        probs = []
        for i, s in enumerate(scores):
            sk = sink_ref[kp_id * (2 * G_A) + kvl * G_A + i]
            s = jnp.where(valid, s, NEG_INF)
            m = jnp.maximum(jnp.max(s, axis=-1, keepdims=True), sk)
            p = jnp.exp(s - m)
            den = jnp.sum(p, axis=-1, keepdims=True) + jnp.exp(sk - m)
            probs.append((p.astype(BF16), 1.0 / den))
        for c in range(G_A // 2):
            col = (kvl * (G_A // 2) + c) * LANES
            acc = jnp.zeros((BLOCK, LANES), F32)
            for par in range(2):
                p, rden = probs[2 * c + par]
                acc = acc + jnp.dot(p, vb[par], preferred_element_type=F32) * rden
            o_ref[:, col:col + LANES] = acc


def _attn_a(p4, sink):
    _, b, s, _ = p4.shape
    nb = s // BLOCK
    qb = 2 * G_A * HEAD_DIM // LANES
    grid = (b, N_KV_A // 2, nb)
    prev = lambda n: jnp.maximum(n - 1, 0)
    return pl.pallas_call(
        _attn_a_kernel,
        grid=grid,
        in_specs=[
            pl.BlockSpec(memory_space=pltpu.SMEM),
            pl.BlockSpec((qb, None, BLOCK, LANES), lambda bi, kp, n: (kp, bi, n, 0)),
            pl.BlockSpec((None, None, BLOCK, LANES), lambda bi, kp, n: (COL_KA + kp, bi, prev(n), 0)),
            pl.BlockSpec((None, None, BLOCK, LANES), lambda bi, kp, n: (COL_KA + kp, bi, n, 0)),
            pl.BlockSpec((None, None, BLOCK, LANES), lambda bi, kp, n: (COL_VA + kp, bi, prev(n), 0)),
            pl.BlockSpec((None, None, BLOCK, LANES), lambda bi, kp, n: (COL_VA + kp, bi, n, 0)),
        ],
        out_specs=pl.BlockSpec((None, BLOCK, qb * LANES), lambda bi, kp, n: (bi, n, kp)),
        out_shape=jax.ShapeDtypeStruct((b, s, WIDTH_A), F32),
        compiler_params=_cparams(("arbitrary", "arbitrary", "arbitrary")),
        name="attn_a",
    )(sink, p4, p4, p4, p4, p4)


def _attn_b_kernel(q_ref, k_ref, v_ref, o_ref, oc_ref, lc_ref):
    seq = q_ref.shape[0]
    lane = lax.broadcasted_iota(jnp.int32, (1, LANES), 1)
    lo = lane < HEAD_DIM
    nt = (((1,), (1,)), ((), ()))

    masks = (lo, jnp.logical_not(lo))

    def blocks(cfg, d, starts, has_prev):
        nk = 2 * BLOCK if has_prev else BLOCK
        qi = lax.broadcasted_iota(jnp.int32, (BLOCK, nk), 0)
        kj = lax.broadcasted_iota(jnp.int32, (BLOCK, nk), 1)
        if has_prev:
            valid = (kj >= qi) & (kj <= qi + BLOCK)
        else:
            valid = kj <= qi
        rows_qs, vs, scores = [], [], []
        for start in starts:
            rows_q = pl.ds(start, BLOCK, stride=d) if d > 1 else pl.ds(start, BLOCK)
            kstart = start - BLOCK * d if has_prev else start
            rows_k = pl.ds(kstart, nk, stride=d) if d > 1 else pl.ds(kstart, nk)
            q = q_ref[rows_q, :].astype(BF16)
            k = k_ref[rows_k, :]
            rows_qs.append(rows_q)
            vs.append(v_ref[rows_k, :])
            for msk in masks:
                kpar = jnp.where(msk, k, 0.0).astype(BF16)
                scores.append(lax.dot_general(q, kpar, nt, preferred_element_type=F32) * SCALE)
        probs = []
        for s in scores:
            s = jnp.where(valid, s, NEG_INF)
            m = jnp.max(s, axis=-1, keepdims=True)
            p = jnp.exp(s - m)
            den = jnp.sum(p, axis=-1, keepdims=True)
            probs.append((p.astype(BF16), 1.0 / den, jnp.log(den) + m))
        for b, (rows_q, v) in enumerate(zip(rows_qs, vs)):
            acc = jnp.zeros((BLOCK, LANES), F32)
            lse = jnp.zeros((BLOCK, LANES), F32)
            for par, msk in enumerate(masks):
                p, rden, l = probs[2 * b + par]
                vpar = jnp.where(msk, v, 0.0).astype(BF16)
                acc = acc + jnp.dot(p, vpar, preferred_element_type=F32) * rden
                lse = jnp.where(msk, l, lse)
            oc_ref[cfg, rows_q, :] = acc
            lc_ref[cfg, rows_q, :] = lse

    def block(cfg, d, start, has_prev):
        blocks(cfg, d, [start], has_prev)

    for cfg, (w, d) in enumerate(DILATED):
        assert w // d == BLOCK
        nblk = seq // d // BLOCK
        span = BLOCK * d
        if d == 1:
            block(cfg, d, 0, False)
            cnt = nblk - 1
            group = next(gs for gs in (5, 4, 3, 2, 1) if cnt % gs == 0)

            def body_seq(it, carry, cfg=cfg, d=d, span=span, group=group):
                blocks(cfg, d, [(1 + it * group + u) * span for u in range(group)], True)
                return carry

            lax.fori_loop(0, cnt // group, body_seq, 0)
            continue
        cg = min(d, 4)
        ncg = d // cg

        def body_first(it, carry, cfg=cfg, d=d, cg=cg):
            blocks(cfg, d, [it * cg + u for u in range(cg)], False)
            return carry

        if ncg == 1:
            body_first(0, 0)
        else:
            lax.fori_loop(0, ncg, body_first, 0)
        if nblk > 1:
            def body_rest(it, carry, cfg=cfg, d=d, cg=cg, ncg=ncg, span=span):
                n = 1 + it // ncg
                rc = it % ncg
                blocks(cfg, d, [rc * cg + u + n * span for u in range(cg)], True)
                return carry

            lax.fori_loop(0, (nblk - 1) * ncg, body_rest, 0)

    def combine(i, carry):
        rows = pl.ds(pl.multiple_of(i * BLOCK, BLOCK), BLOCK)
        l0 = lc_ref[0, rows, :]
        l1 = lc_ref[1, rows, :]
        l2 = lc_ref[2, rows, :]
        mx = jnp.maximum(jnp.maximum(l0, l1), l2)
        w0 = jnp.exp(l0 - mx)
        w1 = jnp.exp(l1 - mx)
        w2 = jnp.exp(l2 - mx)
        num = w0 * oc_ref[0, rows, :] + w1 * oc_ref[1, rows, :] + w2 * oc_ref[2, rows, :]
        o_ref[rows, :] = num / (w0 + w1 + w2)
        return carry

    lax.fori_loop(0, seq // BLOCK, combine, 0)


def _attn_b(p4):
    _, b, s, _ = p4.shape
    grid = (b, N_HEADS_B // 2)
    ncfg = len(DILATED)
    return pl.pallas_call(
        _attn_b_kernel,
        grid=grid,
        in_specs=[
            pl.BlockSpec((None, None, s, LANES), lambda bi, hp: (COL_QB + hp, bi, 0, 0)),
            pl.BlockSpec((None, None, s, LANES), lambda bi, hp: (COL_KB + hp, bi, 0, 0)),
            pl.BlockSpec((None, None, s, LANES), lambda bi, hp: (COL_VB + hp, bi, 0, 0)),
        ],
        out_specs=pl.BlockSpec((None, s, LANES), lambda bi, hp: (bi, 0, hp)),
        out_shape=jax.ShapeDtypeStruct((b, s, WIDTH_B), F32),
        scratch_shapes=[pltpu.VMEM((ncfg, s, LANES), F32), pltpu.VMEM((ncfg, s, LANES), F32)],
        compiler_params=_cparams(("arbitrary", "arbitrary")),
        name="attn_b",
    )(p4, p4, p4)


SAMP_A_BB = 8


def _samp_a_kernel(q_ref, kt_ref, vt_ref, kn_ref, vn_ref, sink_ref, o_ref):
    nt = (((1,), (1,)), ((), ()))
    for bb in range(SAMP_A_BB):
        for kv in range(N_KV_A):
            q = q_ref[bb, kv * G_A:(kv + 1) * G_A, :]
            kt = kt_ref[bb, kv]
            vt = vt_ref[bb, kv]
            kn = kn_ref[bb, kv:kv + 1, :]
            vn = vn_ref[bb, kv:kv + 1, :]
            sk = sink_ref[kv]
            s = jnp.dot(q.astype(BF16), kt.astype(BF16), preferred_element_type=F32) * SCALE
            sn = jnp.sum(q * kn, axis=-1, keepdims=True) * SCALE
            m = jnp.maximum(jnp.maximum(jnp.max(s, axis=-1, keepdims=True), sn), sk)
            p = jnp.exp(s - m)
            pn = jnp.exp(sn - m)
            den = jnp.sum(p, axis=-1, keepdims=True) + pn + jnp.exp(sk - m)
            o = lax.dot_general(p.astype(BF16), vt.astype(BF16), nt, preferred_element_type=F32)
            o_ref[bb, kv * G_A:(kv + 1) * G_A, :] = (o + pn * vn) / den


def _samp_a(qa_s, kt_a, vt_a, kn, vn, sink3):
    db = qa_s.shape[0]
    lb = kt_a.shape[-1]
    assert lb <= WINDOW_A
    bb = SAMP_A_BB
    return pl.pallas_call(
        _samp_a_kernel,
        grid=(db // bb,),
        in_specs=[
            pl.BlockSpec((bb, N_HEADS_A, HEAD_DIM), lambda b: (b, 0, 0)),
            pl.BlockSpec((bb, N_KV_A, HEAD_DIM, lb), lambda b: (b, 0, 0, 0)),
            pl.BlockSpec((bb, N_KV_A, HEAD_DIM, lb), lambda b: (b, 0, 0, 0)),
            pl.BlockSpec((bb, N_KV_A, HEAD_DIM), lambda b: (b, 0, 0)),
            pl.BlockSpec((bb, N_KV_A, HEAD_DIM), lambda b: (b, 0, 0)),
            pl.BlockSpec((N_KV_A, G_A, 1), lambda b: (0, 0, 0)),
        ],
        out_specs=pl.BlockSpec((bb, N_HEADS_A, HEAD_DIM), lambda b: (b, 0, 0)),
        out_shape=jax.ShapeDtypeStruct((db, N_HEADS_A, HEAD_DIM), F32),
        compiler_params=_cparams(("arbitrary",)),
        name="samp_a",
    )(qa_s, kt_a, vt_a, kn, vn, sink3)


SAMP_B_HG = 8


def _samp_b_kernel(kt_ref, vt_ref, qt_ref, knt_ref, vnt_ref, o_ref):
    b = pl.program_id(1)
    lb = kt_ref.shape[-1]
    laneb = lax.broadcasted_iota(jnp.int32, (1, LANES), 1) == b

    def column(ref):
        return jnp.sum(jnp.where(laneb, ref[...], 0.0), axis=-1, keepdims=True)

    qcol = column(qt_ref)
    kncol = column(knt_ref)
    vncol = column(vnt_ref)
    hrow = lax.broadcasted_iota(jnp.int32, (SAMP_B_HG, 1), 0)
    s = jnp.zeros((SAMP_B_HG, lb), F32)
    sn = jnp.zeros((SAMP_B_HG, 1), F32)
    for h in range(SAMP_B_HG):
        hs = slice(h * HEAD_DIM, (h + 1) * HEAD_DIM)
        s_h = jnp.sum(kt_ref[0, h] * qcol[hs], axis=0, keepdims=True)
        sn_h = jnp.sum(qcol[hs] * kncol[hs], axis=0, keepdims=True)
        s = jnp.where(hrow == h, s_h, s)
        sn = jnp.where(hrow == h, sn_h, sn)
    s = s * SCALE
    sn = sn * SCALE
    dist = lb - lax.broadcasted_iota(jnp.int32, (1, lb), 1)
    ps, pns, lses = [], [], []
    for w, d in DILATED:
        valid = ((dist % d) == 0) & (dist <= w)
        m = jnp.maximum(jnp.max(jnp.where(valid, s, NEG_INF), axis=-1, keepdims=True), sn)
        p = jnp.where(valid, jnp.exp(s - m), 0.0)
        pn = jnp.exp(sn - m)
        den = jnp.sum(p, axis=-1, keepdims=True) + pn
        ps.append(p / den)
        pns.append(pn / den)
        lses.append(jnp.log(den) + m)
    mx = functools.reduce(jnp.maximum, lses)
    ws = [jnp.exp(l - mx) for l in lses]
    wsum = functools.reduce(lambda a, c: a + c, ws)
    pmix = functools.reduce(lambda a, c: a + c, [w_ * p_ for w_, p_ in zip(ws, ps)]) / wsum
    pnmix = functools.reduce(lambda a, c: a + c, [w_ * p_ for w_, p_ in zip(ws, pns)]) / wsum
    cols = []
    for h in range(SAMP_B_HG):
        hs = slice(h * HEAD_DIM, (h + 1) * HEAD_DIM)
        oc = jnp.sum(vt_ref[0, h] * pmix[h:h + 1, :], axis=-1, keepdims=True)
        cols.append(oc + pnmix[h:h + 1, :] * vncol[hs])
    ocol = jnp.concatenate(cols, axis=0)

    @pl.when(b == 0)
    def _():
        o_ref[...] = jnp.zeros_like(o_ref)

    o_ref[...] = jnp.where(laneb, ocol, o_ref[...])


def _samp_b(kt_b, vt_b, qt, knt, vnt):
    db, nh, hd, lb = kt_b.shape
    assert db == LANES
    rows = SAMP_B_HG * HEAD_DIM
    grid = (nh // SAMP_B_HG, db)
    return pl.pallas_call(
        _samp_b_kernel,
        grid=grid,
        in_specs=[
            pl.BlockSpec((1, SAMP_B_HG, hd, lb), lambda g, b: (b, g, 0, 0)),
            pl.BlockSpec((1, SAMP_B_HG, hd, lb), lambda g, b: (b, g, 0, 0)),
            pl.BlockSpec((rows, db), lambda g, b: (g, 0)),
            pl.BlockSpec((rows, db), lambda g, b: (g, 0)),
            pl.BlockSpec((rows, db), lambda g, b: (g, 0)),
        ],
        out_specs=pl.BlockSpec((rows, db), lambda g, b: (g, 0)),
        out_shape=jax.ShapeDtypeStruct((nh * hd, db), F32),
        compiler_params=_cparams(("arbitrary", "arbitrary")),
        name="samp_b",
    )(kt_b, vt_b, qt, knt, vnt)


MERGE_TN = 1024
MERGE_NORM_ROWS = 512


def _merge_kernel(oa_ref, ob_ref, x_ref, ga_ref, gb_ref, w_ref, gfc_ref, ht_ref, xt_ref, cat_ref, ssq_ref):
    j = pl.program_id(1)
    nj = pl.num_programs(1)

    @pl.when(j == 0)
    def _():
        oa = oa_ref[...]
        ob = ob_ref[...]
        ya = (oa * lax.rsqrt(jnp.mean(oa * oa, axis=-1, keepdims=True) + EPS)) * ga_ref[...]
        yb = (ob * lax.rsqrt(jnp.mean(ob * ob, axis=-1, keepdims=True) + EPS)) * gb_ref[...]
        cat_ref[:, :WIDTH_A] = ya.astype(BF16)
        cat_ref[:, WIDTH_A:] = yb.astype(BF16)
        ssq_ref[...] = jnp.zeros_like(ssq_ref)

    h_t = (x_ref[...] + jnp.dot(cat_ref[...], w_ref[...], preferred_element_type=F32)).T
    ht_ref[pl.ds(pl.multiple_of(j * MERGE_TN, MERGE_TN), MERGE_TN), :] = h_t
    ssq_ref[...] += jnp.sum(h_t * h_t, axis=0, keepdims=True)

    @pl.when(j == nj - 1)
    def _():
        rinv = lax.rsqrt(ssq_ref[...] * (1.0 / D_MODEL) + EPS)
        for c in range(D_MODEL // MERGE_NORM_ROWS):
            rs = slice(c * MERGE_NORM_ROWS, (c + 1) * MERGE_NORM_ROWS)
            xt_ref[rs, :] = ((ht_ref[rs, :] * rinv) * gfc_ref[rs, :]).astype(BF16)


def _merge(oa, ob, x2d, ga, gb, w_bf, gf_col, tm):
    t = x2d.shape[0]
    nj = D_MODEL // MERGE_TN
    grid = (t // tm, nj)
    once = pl.Buffered(1)
    return pl.pallas_call(
        _merge_kernel,
        grid=grid,
        in_specs=[
            pl.BlockSpec((tm, WIDTH_A), lambda i, j: (i, 0), pipeline_mode=once),
            pl.BlockSpec((tm, WIDTH_B), lambda i, j: (i, 0), pipeline_mode=once),
            pl.BlockSpec((tm, MERGE_TN), lambda i, j: (i, j)),
            pl.BlockSpec((1, WIDTH_A), lambda i, j: (0, 0)),
            pl.BlockSpec((1, WIDTH_B), lambda i, j: (0, 0)),
            pl.BlockSpec((None, D_MODEL, MERGE_TN), lambda i, j: (j, 0, 0)),
            pl.BlockSpec((D_MODEL, 1), lambda i, j: (0, 0), pipeline_mode=once),
        ],
        out_specs=[
            pl.BlockSpec((D_MODEL, tm), lambda i, j: (0, i), pipeline_mode=once),
            pl.BlockSpec((D_MODEL, tm), lambda i, j: (0, i), pipeline_mode=once),
        ],
        out_shape=[
            jax.ShapeDtypeStruct((D_MODEL, t), F32),
            jax.ShapeDtypeStruct((D_MODEL, t), BF16),
        ],
        scratch_shapes=[pltpu.VMEM((tm, D_MODEL), BF16), pltpu.VMEM((1, tm), F32)],
        compiler_params=_cparams(("arbitrary", "arbitrary")),
        name="merge",
    )(oa, ob, x2d, ga, gb, w_bf, gf_col)


GATE_ROWS = 4


def _top_values(x, k):
    row = lax.broadcasted_iota(jnp.int32, (k, x.shape[1]), 0)
    vals = jnp.zeros((k, x.shape[1]), F32)
    cur = x
    for r in range(k):
        m = jnp.max(cur, axis=0, keepdims=True)
        vals = jnp.where(row == r, m, vals)
        if r + 1 < k:
            cur = jnp.where(cur == m, NEG_INF, cur)
    return vals


def _router_kernel(x_ref, wq_ref, k1_ref, k2_ref, th_ref, e1_ref, s2_ref, e2_ref):
    half = PEER_DKEY // 2
    qt = jnp.dot(wq_ref[...], x_ref[...], preferred_element_type=F32)
    s1_all = jnp.dot(k1_ref[0], qt[:half].astype(BF16), preferred_element_type=F32)
    s2_all = jnp.dot(k2_ref[0], qt[half:].astype(BF16), preferred_element_type=F32)
    for c in range(x_ref.shape[1] // LANES):
        sl = slice(c * LANES, (c + 1) * LANES)
        s1 = s1_all[:, sl]
        s2 = s2_all[:, sl]
        v1 = _top_values(s1, PEER_TOPK)
        v2 = _top_values(s2, PEER_TOPK)
        cands = [v1[0:1] + v2[0:8], v1[0:1] + v2[8:16]]
        cands += [v1[a:a + 1] + v2[0:8] for a in range(1, 8)]
        cands += [v1[8:16] + v2[0:1]]
        cand = jnp.concatenate(cands, axis=0)
        tk = _top_values(cand, PEER_TOPK)[PEER_TOPK - 1:PEER_TOPK]
        m1 = v1[0:1]
        m2 = v2[0:1]
        z = jnp.sum(jnp.where(cand >= tk, jnp.exp(cand - (m1 + m2)), 0.0), axis=0, keepdims=True)
        theta = jnp.full(s1.shape, jnp.inf, F32)
        for b in range(PEER_TOPK):
            vb = v2[b:b + 1]
            theta = jnp.where((s1 + vb) >= tk, vb, theta)
        theta = jnp.where(s1 >= v1[PEER_TOPK - 1:PEER_TOPK], theta, jnp.inf)
        e1 = jnp.exp(s1 - m1) / z
        for grp in range(PEER_NKEYS // GATE_ROWS):
            rows = slice(grp * GATE_ROWS, (grp + 1) * GATE_ROWS)
            th_ref[0, grp, :, sl] = theta[rows]
            e1_ref[0, grp, :, sl] = e1[rows]
        s2_ref[0, :, sl] = s2
        e2_ref[0, :, sl] = jnp.exp(s2 - m2)


def _router(xt, wqt, k1b, k2b, tm):
    t = xt.shape[1]
    grid = (t // tm, PEER_HEADS)
    out = jax.ShapeDtypeStruct((PEER_HEADS, PEER_NKEYS, t), F32)
    ospec = pl.BlockSpec((1, PEER_NKEYS, tm), lambda i, h: (h, 0, i))
    ngrp = PEER_NKEYS // GATE_ROWS
    gout = jax.ShapeDtypeStruct((PEER_HEADS, ngrp, GATE_ROWS, t), F32)
    gspec = pl.BlockSpec((1, ngrp, GATE_ROWS, tm), lambda i, h: (h, 0, 0, i))
    return pl.pallas_call(
        _router_kernel,
        grid=grid,
        in_specs=[
            pl.BlockSpec((D_MODEL, tm), lambda i, h: (0, i)),
            pl.BlockSpec((PEER_DKEY, D_MODEL), lambda i, h: (h, 0)),
            pl.BlockSpec((1, PEER_NKEYS, PEER_DKEY // 2), lambda i, h: (h, 0, 0)),
            pl.BlockSpec((1, PEER_NKEYS, PEER_DKEY // 2), lambda i, h: (h, 0, 0)),
        ],
        out_specs=[gspec, gspec, ospec, ospec],
        out_shape=[gout, gout, out, out],
        compiler_params=_cparams(("arbitrary", "arbitrary")),
        name="router",
    )(xt, wqt, k1b, k2b)


EXP_EB = 512
EXP_IB = EXP_EB // PEER_NKEYS
assert EXP_IB == GATE_ROWS


def _experts_kernel(x_ref, u_ref, vt_ref, th_ref, e1_ref, s2_ref, e2_ref, o_ref, at_ref):
    e = pl.program_id(1)

    @pl.when(e == 0)
    def _():
        o_ref[...] = jnp.zeros_like(o_ref)

    ht = jnp.dot(u_ref[...], x_ref[...], preferred_element_type=F32)
    for il in range(EXP_IB):
        rs = slice(il * PEER_NKEYS, (il + 1) * PEER_NKEYS)
        for c in range(x_ref.shape[1] // LANES):
            sl = slice(c * LANES, (c + 1) * LANES)
            g = jnp.zeros((PEER_NKEYS, LANES), F32)
            for h in range(PEER_HEADS):
                th = th_ref[h, 0, il:il + 1, sl]
                e1 = e1_ref[h, 0, il:il + 1, sl]
                g = g + jnp.where(s2_ref[h, :, sl] >= th, e2_ref[h, :, sl], 0.0) * e1
            at_ref[rs, sl] = (jax.nn.gelu(ht[rs, sl]) * g).astype(BF16)
    o_ref[...] += jnp.dot(vt_ref[...], at_ref[...], preferred_element_type=F32)


def _experts(xt, u_bf, vt_bf, th4, e14, s2, e2, tm):
    t = xt.shape[1]
    grid = (t // tm, PEER_EXPERTS // EXP_EB)
    return pl.pallas_call(
        _experts_kernel,
        grid=grid,
        in_specs=[
            pl.BlockSpec((D_MODEL, tm), lambda i, e: (0, i)),
            pl.BlockSpec((EXP_EB, D_MODEL), lambda i, e: (e, 0)),
            pl.BlockSpec((D_MODEL, EXP_EB), lambda i, e: (0, e)),
            pl.BlockSpec((PEER_HEADS, 1, EXP_IB, tm), lambda i, e: (0, e, 0, i)),
            pl.BlockSpec((PEER_HEADS, 1, EXP_IB, tm), lambda i, e: (0, e, 0, i)),
            pl.BlockSpec((PEER_HEADS, PEER_NKEYS, tm), lambda i, e: (0, 0, i)),
            pl.BlockSpec((PEER_HEADS, PEER_NKEYS, tm), lambda i, e: (0, 0, i)),
        ],
        out_specs=pl.BlockSpec((D_MODEL, tm), lambda i, e: (0, i)),
        out_shape=jax.ShapeDtypeStruct((D_MODEL, t), F32),
        scratch_shapes=[pltpu.VMEM((EXP_EB, tm), BF16)],
        compiler_params=_cparams(("arbitrary", "arbitrary")),
        name="experts",
    )(xt, u_bf, vt_bf, th4, e14, s2, e2)


CAST_ROWS = 1024
CAST_COLS = 2048


def _cast_t_kernel(v_ref, o_ref):
    o_ref[...] = v_ref[...].T.astype(BF16)


def _cast_transposed(v):
    e, d = v.shape
    return pl.pallas_call(
        _cast_t_kernel,
        grid=(e // CAST_ROWS, d // CAST_COLS),
        in_specs=[pl.BlockSpec((CAST_ROWS, CAST_COLS), lambda i, j: (i, j))],
        out_specs=pl.BlockSpec((CAST_COLS, CAST_ROWS), lambda i, j: (j, i)),
        out_shape=jax.ShapeDtypeStruct((d, e), BF16),
        compiler_params=_cparams(("arbitrary", "arbitrary")),
        name="cast_t",
    )(v)


def _final_kernel(ht_ref, ft_ref, g_ref, y_ref):
    h = ht_ref[...] + ft_ref[...]
    ms = jnp.mean(h * h, axis=0, keepdims=True)
    y = (h * lax.rsqrt(ms + EPS)) * g_ref[...]
    y_ref[...] = y.T


def _final(ht, ft, g_col, tm):
    t = ht.shape[1]
    return pl.pallas_call(
        _final_kernel,
        grid=(t // tm,),
        in_specs=[
            pl.BlockSpec((D_MODEL, tm), lambda i: (0, i)),
            pl.BlockSpec((D_MODEL, tm), lambda i: (0, i)),
            pl.BlockSpec((D_MODEL, 1), lambda i: (0, 0)),
        ],
        out_specs=pl.BlockSpec((tm, D_MODEL), lambda i: (i, 0)),
        out_shape=jax.ShapeDtypeStruct((t, D_MODEL), F32),
        compiler_params=_cparams(("arbitrary",)),
        name="final",
    )(ht, ft, g_col)


def kernel(x_prompt, x_sample, state_a_k, state_a_v, state_b_k, state_b_v, g_attn, w_in, attn_sink,
           g_out_a, g_out_b, w_out, g_ffn, peer_wq, peer_k1, peer_k2, peer_u, peer_v, g_final):
    bsz, seq, _ = x_prompt.shape
    db, dt, _ = x_sample.shape
    assert w_in.shape[0] == 1 and dt == 1
    past = PAST_LEN
    lb = state_b_k.shape[2]
    assert lb == max(w for w, _ in DILATED)

    def column_blocks(w, tn):
        k, n = w.shape
        return jnp.transpose(w.astype(BF16).reshape(k, n // tn, tn), (1, 0, 2))

    w_in_b = column_blocks(w_in[0], PROJ_TN)
    w_out_b = column_blocks(w_out[0], MERGE_TN)
    wq_t = peer_wq[0].reshape(D_MODEL, PEER_HEADS * PEER_DKEY).T.astype(BF16)
    k1_b = peer_k1[0].astype(BF16)
    k2_b = peer_k2[0].astype(BF16)
    u_b = peer_u[0].astype(BF16)
    vt_b = _cast_transposed(peer_v[0])

    rope_flags = jnp.concatenate([
        jnp.ones((1, WIDTH_A + N_KV_A * HEAD_DIM), F32), jnp.zeros((1, N_KV_A * HEAD_DIM), F32),
        jnp.ones((1, 2 * WIDTH_B), F32), jnp.zeros((1, WIDTH_B), F32)], axis=-1)
    cos_p, sin_p = _rope_tables(jnp.arange(seq))
    cos_s, sin_s = _rope_tables(jnp.full((db,), past))

    tm_proj, tm_merge = 1024, 512
    gf_col = g_ffn.reshape(D_MODEL, 1)
    xp = x_prompt.reshape(bsz * seq, D_MODEL)
    pp = _proj(xp, g_attn, w_in_b, cos_p, sin_p, rope_flags, tm_proj, seq // tm_proj)
    pp4 = pp.reshape(QKV_COLS // LANES, bsz, seq, LANES)
    oa_p = _attn_a(pp4, attn_sink[0])
    ob_p = _attn_b(pp4)
    ht_p, xt_p = _merge(oa_p.reshape(bsz * seq, WIDTH_A), ob_p.reshape(bsz * seq, WIDTH_B), xp,
                        g_out_a, g_out_b, w_out_b, gf_col, tm_merge)

    xs = x_sample.reshape(db, D_MODEL)
    ps = _proj(xs, g_attn, w_in_b, cos_s, sin_s, rope_flags, db, 1)

    def sample_cols(lo, hi):
        return jnp.transpose(ps[lo:hi], (1, 0, 2)).reshape(db, (hi - lo) * LANES)

    qa_s = sample_cols(COL_QA, COL_KA).reshape(db, N_HEADS_A, HEAD_DIM)
    ka_s = sample_cols(COL_KA, COL_VA).reshape(db, N_KV_A, HEAD_DIM)
    va_s = sample_cols(COL_VA, COL_QB).reshape(db, N_KV_A, HEAD_DIM)
    kb_s = sample_cols(COL_KB, COL_VB)
    vb_s = sample_cols(COL_VB, QKV_COLS // LANES)
    kt_a = jnp.transpose(state_a_k[0], (0, 2, 3, 1))
    vt_a = jnp.transpose(state_a_v[0], (0, 2, 3, 1))
    skt_b = jnp.transpose(state_b_k[0], (0, 2, 3, 1))
    svt_b = jnp.transpose(state_b_v[0], (0, 2, 3, 1))
    oa_s = _samp_a(qa_s, kt_a, vt_a, ka_s, va_s, attn_sink[0].reshape(N_KV_A, G_A, 1))
    obt_s = _samp_b(skt_b, svt_b, sample_cols(COL_QB, COL_KB).T, kb_s.T, vb_s.T)
    ht_s, xt_s = _merge(oa_s.reshape(db, WIDTH_A), obt_s.T, xs, g_out_a, g_out_b, w_out_b, gf_col, db)

    g_col = g_final.reshape(D_MODEL, 1)

    def peer_and_final(ht, xt, tm_router, tm_experts, tm_final):
        th, e1, s2, e2 = _router(xt, wq_t, k1_b, k2_b, tm_router)
        ft = _experts(xt, u_b, vt_b, th, e1, s2, e2, tm_experts)
        return _final(ht, ft, g_col, tm_final)

    y_p = peer_and_final(ht_p, xt_p, 1024, 512, 256).reshape(bsz, seq, D_MODEL)
    y_s = peer_and_final(ht_s, xt_s, db, db, db).reshape(db, 1, D_MODEL)

    def prompt_cols(lo, hi, first_row, heads):
        blk = jnp.transpose(pp4[lo:hi, :, first_row:], (1, 2, 0, 3))
        return blk.reshape(1, bsz, seq - first_row, heads, HEAD_DIM)

    rows_a = min(WINDOW_A, seq)
    ka_p = prompt_cols(COL_KA, COL_VA, seq - rows_a, N_KV_A)
    va_p = prompt_cols(COL_VA, COL_QB, seq - rows_a, N_KV_A)
    kb_p = prompt_cols(COL_KB, COL_VB, 0, N_HEADS_B)
    vb_p = prompt_cols(COL_VB, QKV_COLS // LANES, 0, N_HEADS_B)
    return (y_p, y_s, ka_p, va_p, kb_p, vb_p,
            ka_s.reshape(1, db, 1, N_KV_A, HEAD_DIM), va_s.reshape(1, db, 1, N_KV_A, HEAD_DIM),
            kb_s.reshape(1, db, 1, N_HEADS_B, HEAD_DIM), vb_s.reshape(1, db, 1, N_HEADS_B, HEAD_DIM))
```

```python
import functools
import math

import jax
import jax.numpy as jnp
import numpy as np
from jax import lax
from jax.experimental import pallas as pl
from jax.experimental.pallas import tpu as pltpu

D_MODEL = 4096
HEAD_DIM = 64
WIDTH_A = 2048
WIDTH_B = 2048
N_HEADS_A = 32
N_KV_A = 4
G_A = 8
N_HEADS_B = 32
WINDOW_A = 128
DILATED = ((128, 1), (512, 4), (2048, 16))
BLOCK = 128
PAST_LEN = 8192
ROPE_THETA = 10000.0
SCALE = HEAD_DIM ** -0.5
PEER_HEADS = 8
PEER_NKEYS = 128
PEER_EXPERTS = PEER_NKEYS * PEER_NKEYS
PEER_DKEY = 256
PEER_TOPK = 16
EPS = 1e-6
QKV_COLS = WIDTH_A + 2 * N_KV_A * HEAD_DIM + 3 * WIDTH_B

COL_QA = 0
COL_KA = WIDTH_A // 128
COL_VA = COL_KA + N_KV_A * HEAD_DIM // 128
COL_QB = COL_VA + N_KV_A * HEAD_DIM // 128
COL_KB = COL_QB + WIDTH_B // 128
COL_VB = COL_KB + WIDTH_B // 128

LANES = 128
VMEM_LIMIT = 60 * 1024 * 1024

BF16 = jnp.bfloat16
F32 = jnp.float32
NEG_INF = float("-inf")


def _cparams(sem):
    return pltpu.CompilerParams(dimension_semantics=sem, vmem_limit_bytes=VMEM_LIMIT)


PROJ_TN = 512


def _proj_kernel(x_ref, g_ref, w_ref, cos_ref, sin_ref, rope_ref, o_ref, xn_ref):
    j = pl.program_id(1)

    @pl.when(j == 0)
    def _():
        x = x_ref[...]
        ms = jnp.mean(x * x, axis=-1, keepdims=True)
        xn_ref[...] = ((x * lax.rsqrt(ms + EPS)) * g_ref[...]).astype(BF16)

    p = jnp.dot(xn_ref[...], w_ref[...], preferred_element_type=F32)

    lane = lax.broadcasted_iota(jnp.int32, (1, LANES), 1)
    first_half = (lane % HEAD_DIM) < (HEAD_DIM // 2)
    cos = cos_ref[...]
    sin = sin_ref[...]
    for c in range(PROJ_TN // LANES):
        sl = slice(c * LANES, (c + 1) * LANES)
        pc = p[:, sl]
        partner = jnp.where(first_half, pltpu.roll(pc, LANES - HEAD_DIM // 2, 1),
                            pltpu.roll(pc, HEAD_DIM // 2, 1))
        roped = pc * cos + partner * sin
        o_ref[c] = jnp.where(rope_ref[:, sl] > 0.0, roped, pc)


def _proj(x2d, g, w_bf, cos_t, sin_t, rope_flags, tm, pos_blocks):
    t = x2d.shape[0]
    grid = (t // tm, QKV_COLS // PROJ_TN)
    return pl.pallas_call(
        _proj_kernel,
        grid=grid,
        in_specs=[
            pl.BlockSpec((tm, D_MODEL), lambda i, j: (i, 0), pipeline_mode=pl.Buffered(1)),
            pl.BlockSpec((1, D_MODEL), lambda i, j: (0, 0)),
            pl.BlockSpec((D_MODEL, PROJ_TN), lambda i, j: (0, j)),
            pl.BlockSpec((tm, LANES), lambda i, j: (i % pos_blocks, 0)),
            pl.BlockSpec((tm, LANES), lambda i, j: (i % pos_blocks, 0)),
            pl.BlockSpec((1, PROJ_TN), lambda i, j: (0, j)),
        ],
        out_specs=pl.BlockSpec((PROJ_TN // LANES, tm, LANES), lambda i, j: (j, i, 0)),
        out_shape=jax.ShapeDtypeStruct((QKV_COLS // LANES, t, LANES), F32),
        scratch_shapes=[pltpu.VMEM((tm, D_MODEL), BF16)],
        compiler_params=_cparams(("arbitrary", "arbitrary")),
        name="proj",
    )(x2d, g, w_bf, cos_t, sin_t, rope_flags)


def _rope_tables(pos):
    half = HEAD_DIM // 2
    inv = ROPE_THETA ** (-jnp.arange(half, dtype=F32) / half)
    ang = pos.astype(F32)[:, None] * inv[None, :]
    cos = jnp.cos(ang)
    sin = jnp.sin(ang)
    cos_t = jnp.concatenate([cos, cos, cos, cos], axis=-1)
    sin_t = jnp.concatenate([-sin, sin, -sin, sin], axis=-1)
    return cos_t, sin_t


def _attn_a_kernel(sink_ref, q_ref, kp_ref, kc_ref, vp_ref, vc_ref, o_ref):
    kp_id = pl.program_id(1)
    n = pl.program_id(2)
    lane = lax.broadcasted_iota(jnp.int32, (1, LANES), 1)
    lo = lane < HEAD_DIM
    qi = lax.broadcasted_iota(jnp.int32, (BLOCK, 2 * BLOCK), 0)
    kj = lax.broadcasted_iota(jnp.int32, (BLOCK, 2 * BLOCK), 1)
    first_key = jnp.where(n == 0, BLOCK, 0)
    valid = (kj >= qi) & (kj <= qi + WINDOW_A) & (kj >= first_key)
    k2 = jnp.concatenate([kp_ref[...], kc_ref[...]], axis=0)
    v2 = jnp.concatenate([vp_ref[...], vc_ref[...]], axis=0)
    nt = (((1,), (1,)), ((), ()))
    for kvl in range(2):
        if kvl == 0:
            k_lo = jnp.where(lo, k2, 0.0)
            k_hi = pltpu.roll(k_lo, HEAD_DIM, 1)
            v_lo = jnp.where(lo, v2, 0.0)
            v_hi = pltpu.roll(v_lo, HEAD_DIM, 1)
        else:
            k_hi = jnp.where(lo, 0.0, k2)
            k_lo = pltpu.roll(k_hi, HEAD_DIM, 1)
            v_hi = jnp.where(lo, 0.0, v2)
            v_lo = pltpu.roll(v_hi, HEAD_DIM, 1)
        kb = (k_lo.astype(BF16), k_hi.astype(BF16))
        vb = (v_lo.astype(BF16), v_hi.astype(BF16))
        scores = []
        for c in range(G_A // 2):
            qc = q_ref[kvl * (G_A // 2) + c].astype(BF16)
            for par in range(2):
                scores.append(lax.dot_general(qc, kb[par], nt, preferred_element_type=F32) * SCALE)
        probs = []
        for i, s in enumerate(scores):
            sk = sink_ref[kp_id * (2 * G_A) + kvl * G_A + i]
            s = jnp.where(valid, s, NEG_INF)
            m = jnp.maximum(jnp.max(s, axis=-1, keepdims=True), sk)
            p = jnp.exp(s - m)
            den = jnp.sum(p, axis=-1, keepdims=True) + jnp.exp(sk - m)
            probs.append((p.astype(BF16), 1.0 / den))
        for c in range(G_A // 2):
            col = (kvl * (G_A // 2) + c) * LANES
            acc = jnp.zeros((BLOCK, LANES), F32)
            for par in range(2):
                p, rden = probs[2 * c + par]
                acc = acc + jnp.dot(p, vb[par], preferred_element_type=F32) * rden
            o_ref[:, col:col + LANES] = acc


def _attn_a(p4, sink):
    _, b, s, _ = p4.shape
    nb = s // BLOCK
    qb = 2 * G_A * HEAD_DIM // LANES
    grid = (b, N_KV_A // 2, nb)
    prev = lambda n: jnp.maximum(n - 1, 0)
    return pl.pallas_call(
        _attn_a_kernel,
        grid=grid,
        in_specs=[
            pl.BlockSpec(memory_space=pltpu.SMEM),
            pl.BlockSpec((qb, None, BLOCK, LANES), lambda bi, kp, n: (kp, bi, n, 0)),
            pl.BlockSpec((None, None, BLOCK, LANES), lambda bi, kp, n: (COL_KA + kp, bi, prev(n), 0)),
            pl.BlockSpec((None, None, BLOCK, LANES), lambda bi, kp, n: (COL_KA + kp, bi, n, 0)),
            pl.BlockSpec((None, None, BLOCK, LANES), lambda bi, kp, n: (COL_VA + kp, bi, prev(n), 0)),
            pl.BlockSpec((None, None, BLOCK, LANES), lambda bi, kp, n: (COL_VA + kp, bi, n, 0)),
        ],
        out_specs=pl.BlockSpec((None, BLOCK, qb * LANES), lambda bi, kp, n: (bi, n, kp)),
        out_shape=jax.ShapeDtypeStruct((b, s, WIDTH_A), F32),
        compiler_params=_cparams(("arbitrary", "arbitrary", "arbitrary")),
        name="attn_a",
    )(sink, p4, p4, p4, p4, p4)


def _attn_b_kernel(q_ref, k_ref, v_ref, o_ref, oc_ref, lc_ref):
    seq = q_ref.shape[0]
    lane = lax.broadcasted_iota(jnp.int32, (1, LANES), 1)
    lo = lane < HEAD_DIM
    nt = (((1,), (1,)), ((), ()))

    masks = (lo, jnp.logical_not(lo))

    def blocks(cfg, d, starts, has_prev):
        nk = 2 * BLOCK if has_prev else BLOCK
        qi = lax.broadcasted_iota(jnp.int32, (BLOCK, nk), 0)
        kj = lax.broadcasted_iota(jnp.int32, (BLOCK, nk), 1)
        if has_prev:
            valid = (kj >= qi) & (kj <= qi + BLOCK)
        else:
            valid = kj <= qi
        rows_qs, vs, scores = [], [], []
        for start in starts:
            rows_q = pl.ds(start, BLOCK, stride=d) if d > 1 else pl.ds(start, BLOCK)
            kstart = start - BLOCK * d if has_prev else start
            rows_k = pl.ds(kstart, nk, stride=d) if d > 1 else pl.ds(kstart, nk)
            q = q_ref[rows_q, :].astype(BF16)
            k = k_ref[rows_k, :]
            rows_qs.append(rows_q)
            vs.append(v_ref[rows_k, :])
            for msk in masks:
                kpar = jnp.where(msk, k, 0.0).astype(BF16)
                scores.append(lax.dot_general(q, kpar, nt, preferred_element_type=F32) * SCALE)
        probs = []
        for s in scores:
            s = jnp.where(valid, s, NEG_INF)
            m = jnp.max(s, axis=-1, keepdims=True)
            p = jnp.exp(s - m)
            den = jnp.sum(p, axis=-1, keepdims=True)
            probs.append((p.astype(BF16), 1.0 / den, jnp.log(den) + m))
        for b, (rows_q, v) in enumerate(zip(rows_qs, vs)):
            acc = jnp.zeros((BLOCK, LANES), F32)
            lse = jnp.zeros((BLOCK, LANES), F32)
            for par, msk in enumerate(masks):
                p, rden, l = probs[2 * b + par]
                vpar = jnp.where(msk, v, 0.0).astype(BF16)
                acc = acc + jnp.dot(p, vpar, preferred_element_type=F32) * rden
                lse = jnp.where(msk, l, lse)
            oc_ref[cfg, rows_q, :] = acc
            lc_ref[cfg, rows_q, :] = lse

    def block(cfg, d, start, has_prev):
        blocks(cfg, d, [start], has_prev)

    for cfg, (w, d) in enumerate(DILATED):
        assert w // d == BLOCK
        nblk = seq // d // BLOCK
        span = BLOCK * d
        if d == 1:
            block(cfg, d, 0, False)
            cnt = nblk - 1
            group = next(gs for gs in (5, 4, 3, 2, 1) if cnt % gs == 0)

            def body_seq(it, carry, cfg=cfg, d=d, span=span, group=group):
                blocks(cfg, d, [(1 + it * group + u) * span for u in range(group)], True)
                return carry

            lax.fori_loop(0, cnt // group, body_seq, 0)
            continue
        cg = min(d, 4)
        ncg = d // cg

        def body_first(it, carry, cfg=cfg, d=d, cg=cg):
            blocks(cfg, d, [it * cg + u for u in range(cg)], False)
            return carry

        if ncg == 1:
            body_first(0, 0)
        else:
            lax.fori_loop(0, ncg, body_first, 0)
        if nblk > 1:
            def body_rest(it, carry, cfg=cfg, d=d, cg=cg, ncg=ncg, span=span):
                n = 1 + it // ncg
                rc = it % ncg
                blocks(cfg, d, [rc * cg + u + n * span for u in range(cg)], True)
                return carry

            lax.fori_loop(0, (nblk - 1) * ncg, body_rest, 0)

    def combine(i, carry):
        rows = pl.ds(pl.multiple_of(i * BLOCK, BLOCK), BLOCK)
        l0 = lc_ref[0, rows, :]
        l1 = lc_ref[1, rows, :]
        l2 = lc_ref[2, rows, :]
        mx = jnp.maximum(jnp.maximum(l0, l1), l2)
        w0 = jnp.exp(l0 - mx)
        w1 = jnp.exp(l1 - mx)
        w2 = jnp.exp(l2 - mx)
        num = w0 * oc_ref[0, rows, :] + w1 * oc_ref[1, rows, :] + w2 * oc_ref[2, rows, :]
        o_ref[rows, :] = num / (w0 + w1 + w2)
        return carry

    lax.fori_loop(0, seq // BLOCK, combine, 0)


def _attn_b(p4):
    _, b, s, _ = p4.shape
    grid = (b, N_HEADS_B // 2)
    ncfg = len(DILATED)
    return pl.pallas_call(
        _attn_b_kernel,
        grid=grid,
        in_specs=[
            pl.BlockSpec((None, None, s, LANES), lambda bi, hp: (COL_QB + hp, bi, 0, 0)),
            pl.BlockSpec((None, None, s, LANES), lambda bi, hp: (COL_KB + hp, bi, 0, 0)),
            pl.BlockSpec((None, None, s, LANES), lambda bi, hp: (COL_VB + hp, bi, 0, 0)),
        ],
        out_specs=pl.BlockSpec((None, s, LANES), lambda bi, hp: (bi, 0, hp)),
        out_shape=jax.ShapeDtypeStruct((b, s, WIDTH_B), F32),
        scratch_shapes=[pltpu.VMEM((ncfg, s, LANES), F32), pltpu.VMEM((ncfg, s, LANES), F32)],
        compiler_params=_cparams(("arbitrary", "arbitrary")),
        name="attn_b",
    )(p4, p4, p4)


SAMP_A_BB = 8


def _samp_a_kernel(q_ref, kt_ref, vt_ref, kn_ref, vn_ref, sink_ref, o_ref):
    nt = (((1,), (1,)), ((), ()))
    for bb in range(SAMP_A_BB):
        for kv in range(N_KV_A):
            q = q_ref[bb, kv * G_A:(kv + 1) * G_A, :]
            kt = kt_ref[bb, kv]
            vt = vt_ref[bb, kv]
            kn = kn_ref[bb, kv:kv + 1, :]
            vn = vn_ref[bb, kv:kv + 1, :]
            sk = sink_ref[kv]
            s = jnp.dot(q.astype(BF16), kt.astype(BF16), preferred_element_type=F32) * SCALE
            sn = jnp.sum(q * kn, axis=-1, keepdims=True) * SCALE
            m = jnp.maximum(jnp.maximum(jnp.max(s, axis=-1, keepdims=True), sn), sk)
            p = jnp.exp(s - m)
            pn = jnp.exp(sn - m)
            den = jnp.sum(p, axis=-1, keepdims=True) + pn + jnp.exp(sk - m)
            o = lax.dot_general(p.astype(BF16), vt.astype(BF16), nt, preferred_element_type=F32)
            o_ref[bb, kv * G_A:(kv + 1) * G_A, :] = (o + pn * vn) / den


def _samp_a(qa_s, kt_a, vt_a, kn, vn, sink3):
    db = qa_s.shape[0]
    lb = kt_a.shape[-1]
    assert lb <= WINDOW_A
    bb = SAMP_A_BB
    return pl.pallas_call(
        _samp_a_kernel,
        grid=(db // bb,),
        in_specs=[
            pl.BlockSpec((bb, N_HEADS_A, HEAD_DIM), lambda b: (b, 0, 0)),
            pl.BlockSpec((bb, N_KV_A, HEAD_DIM, lb), lambda b: (b, 0, 0, 0)),
            pl.BlockSpec((bb, N_KV_A, HEAD_DIM, lb), lambda b: (b, 0, 0, 0)),
            pl.BlockSpec((bb, N_KV_A, HEAD_DIM), lambda b: (b, 0, 0)),
            pl.BlockSpec((bb, N_KV_A, HEAD_DIM), lambda b: (b, 0, 0)),
            pl.BlockSpec((N_KV_A, G_A, 1), lambda b: (0, 0, 0)),
        ],
        out_specs=pl.BlockSpec((bb, N_HEADS_A, HEAD_DIM), lambda b: (b, 0, 0)),
        out_shape=jax.ShapeDtypeStruct((db, N_HEADS_A, HEAD_DIM), F32),
        compiler_params=_cparams(("arbitrary",)),
        name="samp_a",
    )(qa_s, kt_a, vt_a, kn, vn, sink3)


SAMP_B_HG = 8


def _samp_b_kernel(kt_ref, vt_ref, qt_ref, knt_ref, vnt_ref, o_ref):
    b = pl.program_id(1)
    lb = kt_ref.shape[-1]
    laneb = lax.broadcasted_iota(jnp.int32, (1, LANES), 1) == b

    def column(ref):
        return jnp.sum(jnp.where(laneb, ref[...], 0.0), axis=-1, keepdims=True)

    qcol = column(qt_ref)
    kncol = column(knt_ref)
    vncol = column(vnt_ref)
    hrow = lax.broadcasted_iota(jnp.int32, (SAMP_B_HG, 1), 0)
    s = jnp.zeros((SAMP_B_HG, lb), F32)
    sn = jnp.zeros((SAMP_B_HG, 1), F32)
    for h in range(SAMP_B_HG):
        hs = slice(h * HEAD_DIM, (h + 1) * HEAD_DIM)
        s_h = jnp.sum(kt_ref[0, h] * qcol[hs], axis=0, keepdims=True)
        sn_h = jnp.sum(qcol[hs] * kncol[hs], axis=0, keepdims=True)
        s = jnp.where(hrow == h, s_h, s)
        sn = jnp.where(hrow == h, sn_h, sn)
    s = s * SCALE
    sn = sn * SCALE
    dist = lb - lax.broadcasted_iota(jnp.int32, (1, lb), 1)
    ps, pns, lses = [], [], []
    for w, d in DILATED:
        valid = ((dist % d) == 0) & (dist <= w)
        m = jnp.maximum(jnp.max(jnp.where(valid, s, NEG_INF), axis=-1, keepdims=True), sn)
        p = jnp.where(valid, jnp.exp(s - m), 0.0)
        pn = jnp.exp(sn - m)
        den = jnp.sum(p, axis=-1, keepdims=True) + pn
        ps.append(p / den)
        pns.append(pn / den)
        lses.append(jnp.log(den) + m)
    mx = functools.reduce(jnp.maximum, lses)
    ws = [jnp.exp(l - mx) for l in lses]
    wsum = functools.reduce(lambda a, c: a + c, ws)
    pmix = functools.reduce(lambda a, c: a + c, [w_ * p_ for w_, p_ in zip(ws, ps)]) / wsum
    pnmix = functools.reduce(lambda a, c: a + c, [w_ * p_ for w_, p_ in zip(ws, pns)]) / wsum
    cols = []
    for h in range(SAMP_B_HG):
        hs = slice(h * HEAD_DIM, (h + 1) * HEAD_DIM)
        oc = jnp.sum(vt_ref[0, h] * pmix[h:h + 1, :], axis=-1, keepdims=True)
        cols.append(oc + pnmix[h:h + 1, :] * vncol[hs])
    ocol = jnp.concatenate(cols, axis=0)

    @pl.when(b == 0)
    def _():
        o_ref[...] = jnp.zeros_like(o_ref)

    o_ref[...] = jnp.where(laneb, ocol, o_ref[...])


def _samp_b(kt_b, vt_b, qt, knt, vnt):
    db, nh, hd, lb = kt_b.shape
    assert db == LANES
    rows = SAMP_B_HG * HEAD_DIM
    grid = (nh // SAMP_B_HG, db)
    return pl.pallas_call(
        _samp_b_kernel,
        grid=grid,
        in_specs=[
            pl.BlockSpec((1, SAMP_B_HG, hd, lb), lambda g, b: (b, g, 0, 0)),
            pl.BlockSpec((1, SAMP_B_HG, hd, lb), lambda g, b: (b, g, 0, 0)),
            pl.BlockSpec((rows, db), lambda g, b: (g, 0)),
            pl.BlockSpec((rows, db), lambda g, b: (g, 0)),
            pl.BlockSpec((rows, db), lambda g, b: (g, 0)),
        ],
        out_specs=pl.BlockSpec((rows, db), lambda g, b: (g, 0)),
        out_shape=jax.ShapeDtypeStruct((nh * hd, db), F32),
        compiler_params=_cparams(("arbitrary", "arbitrary")),
        name="samp_b",
    )(kt_b, vt_b, qt, knt, vnt)


MERGE_TN = 1024
MERGE_NORM_ROWS = 512


def _merge_kernel(oa_ref, ob_ref, x_ref, ga_ref, gb_ref, w_ref, gfc_ref, ht_ref, xt_ref, cat_ref, ssq_ref):
    j = pl.program_id(1)
    nj = pl.num_programs(1)

    @pl.when(j == 0)
    def _():
        oa = oa_ref[...]
        ob = ob_ref[...]
        ya = (oa * lax.rsqrt(jnp.mean(oa * oa, axis=-1, keepdims=True) + EPS)) * ga_ref[...]
        yb = (ob * lax.rsqrt(jnp.mean(ob * ob, axis=-1, keepdims=True) + EPS)) * gb_ref[...]
        cat_ref[:, :WIDTH_A] = ya.astype(BF16)
        cat_ref[:, WIDTH_A:] = yb.astype(BF16)
        ssq_ref[...] = jnp.zeros_like(ssq_ref)

    h_t = (x_ref[...] + jnp.dot(cat_ref[...], w_ref[...], preferred_element_type=F32)).T
    ht_ref[pl.ds(pl.multiple_of(j * MERGE_TN, MERGE_TN), MERGE_TN), :] = h_t
    ssq_ref[...] += jnp.sum(h_t * h_t, axis=0, keepdims=True)

    @pl.when(j == nj - 1)
    def _():
        rinv = lax.rsqrt(ssq_ref[...] * (1.0 / D_MODEL) + EPS)
        for c in range(D_MODEL // MERGE_NORM_ROWS):
            rs = slice(c * MERGE_NORM_ROWS, (c + 1) * MERGE_NORM_ROWS)
            xt_ref[rs, :] = ((ht_ref[rs, :] * rinv) * gfc_ref[rs, :]).astype(BF16)


def _merge(oa, ob, x2d, ga, gb, w_bf, gf_col, tm):
    t = x2d.shape[0]
    nj = D_MODEL // MERGE_TN
    grid = (t // tm, nj)
    once = pl.Buffered(1)
    return pl.pallas_call(
        _merge_kernel,
        grid=grid,
        in_specs=[
            pl.BlockSpec((tm, WIDTH_A), lambda i, j: (i, 0), pipeline_mode=once),
            pl.BlockSpec((tm, WIDTH_B), lambda i, j: (i, 0), pipeline_mode=once),
            pl.BlockSpec((tm, MERGE_TN), lambda i, j: (i, j)),
            pl.BlockSpec((1, WIDTH_A), lambda i, j: (0, 0)),
            pl.BlockSpec((1, WIDTH_B), lambda i, j: (0, 0)),
            pl.BlockSpec((D_MODEL, MERGE_TN), lambda i, j: (0, j)),
            pl.BlockSpec((D_MODEL, 1), lambda i, j: (0, 0), pipeline_mode=once),
        ],
        out_specs=[
            pl.BlockSpec((D_MODEL, tm), lambda i, j: (0, i), pipeline_mode=once),
            pl.BlockSpec((D_MODEL, tm), lambda i, j: (0, i), pipeline_mode=once),
        ],
        out_shape=[
            jax.ShapeDtypeStruct((D_MODEL, t), F32),
            jax.ShapeDtypeStruct((D_MODEL, t), BF16),
        ],
        scratch_shapes=[pltpu.VMEM((tm, D_MODEL), BF16), pltpu.VMEM((1, tm), F32)],
        compiler_params=_cparams(("arbitrary", "arbitrary")),
        name="merge",
    )(oa, ob, x2d, ga, gb, w_bf, gf_col)


GATE_ROWS = 4


def _top_values(x, k):
    row = lax.broadcasted_iota(jnp.int32, (k, x.shape[1]), 0)
    vals = jnp.zeros((k, x.shape[1]), F32)
    cur = x
    for r in range(k):
        m = jnp.max(cur, axis=0, keepdims=True)
        vals = jnp.where(row == r, m, vals)
        if r + 1 < k:
            cur = jnp.where(cur == m, NEG_INF, cur)
    return vals


def _router_kernel(x_ref, wq_ref, k1_ref, k2_ref, th_ref, e1_ref, s2_ref, e2_ref):
    half = PEER_DKEY // 2
    qt = jnp.dot(wq_ref[...], x_ref[...], preferred_element_type=F32)
    s1_all = jnp.dot(k1_ref[0], qt[:half].astype(BF16), preferred_element_type=F32)
    s2_all = jnp.dot(k2_ref[0], qt[half:].astype(BF16), preferred_element_type=F32)
    for c in range(x_ref.shape[1] // LANES):
        sl = slice(c * LANES, (c + 1) * LANES)
        s1 = s1_all[:, sl]
        s2 = s2_all[:, sl]
        v1 = _top_values(s1, PEER_TOPK)
        v2 = _top_values(s2, PEER_TOPK)
        cands = [v1[0:1] + v2[0:8], v1[0:1] + v2[8:16]]
        cands += [v1[a:a + 1] + v2[0:8] for a in range(1, 8)]
        cands += [v1[8:16] + v2[0:1]]
        cand = jnp.concatenate(cands, axis=0)
        tk = _top_values(cand, PEER_TOPK)[PEER_TOPK - 1:PEER_TOPK]
        m1 = v1[0:1]
        m2 = v2[0:1]
        z = jnp.sum(jnp.where(cand >= tk, jnp.exp(cand - (m1 + m2)), 0.0), axis=0, keepdims=True)
        theta = jnp.full(s1.shape, jnp.inf, F32)
        for b in range(PEER_TOPK):
            vb = v2[b:b + 1]
            theta = jnp.where((s1 + vb) >= tk, vb, theta)
        theta = jnp.where(s1 >= v1[PEER_TOPK - 1:PEER_TOPK], theta, jnp.inf)
        e1 = jnp.exp(s1 - m1) / z
        for grp in range(PEER_NKEYS // GATE_ROWS):
            rows = slice(grp * GATE_ROWS, (grp + 1) * GATE_ROWS)
            th_ref[0, grp, :, sl] = theta[rows]
            e1_ref[0, grp, :, sl] = e1[rows]
        s2_ref[0, :, sl] = s2
        e2_ref[0, :, sl] = jnp.exp(s2 - m2)


def _router(xt, wqt, k1b, k2b, tm):
    t = xt.shape[1]
    grid = (t // tm, PEER_HEADS)
    out = jax.ShapeDtypeStruct((PEER_HEADS, PEER_NKEYS, t), F32)
    ospec = pl.BlockSpec((1, PEER_NKEYS, tm), lambda i, h: (h, 0, i))
    ngrp = PEER_NKEYS // GATE_ROWS
    gout = jax.ShapeDtypeStruct((PEER_HEADS, ngrp, GATE_ROWS, t), F32)
    gspec = pl.BlockSpec((1, ngrp, GATE_ROWS, tm), lambda i, h: (h, 0, 0, i))
    return pl.pallas_call(
        _router_kernel,
        grid=grid,
        in_specs=[
            pl.BlockSpec((D_MODEL, tm), lambda i, h: (0, i)),
            pl.BlockSpec((PEER_DKEY, D_MODEL), lambda i, h: (h, 0)),
            pl.BlockSpec((1, PEER_NKEYS, PEER_DKEY // 2), lambda i, h: (h, 0, 0)),
            pl.BlockSpec((1, PEER_NKEYS, PEER_DKEY // 2), lambda i, h: (h, 0, 0)),
        ],
        out_specs=[gspec, gspec, ospec, ospec],
        out_shape=[gout, gout, out, out],
        compiler_params=_cparams(("arbitrary", "arbitrary")),
        name="router",
    )(xt, wqt, k1b, k2b)


EXP_EB = 512
EXP_IB = EXP_EB // PEER_NKEYS
assert EXP_IB == GATE_ROWS


def _experts_kernel(x_ref, u_ref, vt_ref, th_ref, e1_ref, s2_ref, e2_ref, o_ref, at_ref):
    e = pl.program_id(1)

    @pl.when(e == 0)
    def _():
        o_ref[...] = jnp.zeros_like(o_ref)

    ht = jnp.dot(u_ref[...], x_ref[...], preferred_element_type=F32)
    for il in range(EXP_IB):
        rs = slice(il * PEER_NKEYS, (il + 1) * PEER_NKEYS)
        for c in range(x_ref.shape[1] // LANES):
            sl = slice(c * LANES, (c + 1) * LANES)
            g = jnp.zeros((PEER_NKEYS, LANES), F32)
            for h in range(PEER_HEADS):
                th = th_ref[h, 0, il:il + 1, sl]
                e1 = e1_ref[h, 0, il:il + 1, sl]
                g = g + jnp.where(s2_ref[h, :, sl] >= th, e2_ref[h, :, sl], 0.0) * e1
            at_ref[rs, sl] = (jax.nn.gelu(ht[rs, sl]) * g).astype(BF16)
    o_ref[...] += jnp.dot(vt_ref[...], at_ref[...], preferred_element_type=F32)


def _experts(xt, u_bf, vt_bf, th4, e14, s2, e2, tm):
    t = xt.shape[1]
    grid = (t // tm, PEER_EXPERTS // EXP_EB)
    return pl.pallas_call(
        _experts_kernel,
        grid=grid,
        in_specs=[
            pl.BlockSpec((D_MODEL, tm), lambda i, e: (0, i)),
            pl.BlockSpec((EXP_EB, D_MODEL), lambda i, e: (e, 0)),
            pl.BlockSpec((D_MODEL, EXP_EB), lambda i, e: (0, e)),
            pl.BlockSpec((PEER_HEADS, 1, EXP_IB, tm), lambda i, e: (0, e, 0, i)),
            pl.BlockSpec((PEER_HEADS, 1, EXP_IB, tm), lambda i, e: (0, e, 0, i)),
            pl.BlockSpec((PEER_HEADS, PEER_NKEYS, tm), lambda i, e: (0, 0, i)),
            pl.BlockSpec((PEER_HEADS, PEER_NKEYS, tm), lambda i, e: (0, 0, i)),
        ],
        out_specs=pl.BlockSpec((D_MODEL, tm), lambda i, e: (0, i)),
        out_shape=jax.ShapeDtypeStruct((D_MODEL, t), F32),
        scratch_shapes=[pltpu.VMEM((EXP_EB, tm), BF16)],
        compiler_params=_cparams(("arbitrary", "arbitrary")),
        name="experts",
    )(xt, u_bf, vt_bf, th4, e14, s2, e2)


CAST_ROWS = 1024
CAST_COLS = 2048


def _cast_t_kernel(v_ref, o_ref):
    o_ref[...] = v_ref[...].T.astype(BF16)


def _cast_transposed(v):
    e, d = v.shape
    return pl.pallas_call(
        _cast_t_kernel,
        grid=(e // CAST_ROWS, d // CAST_COLS),
        in_specs=[pl.BlockSpec((CAST_ROWS, CAST_COLS), lambda i, j: (i, j))],
        out_specs=pl.BlockSpec((CAST_COLS, CAST_ROWS), lambda i, j: (j, i)),
        out_shape=jax.ShapeDtypeStruct((d, e), BF16),
        compiler_params=_cparams(("arbitrary", "arbitrary")),
        name="cast_t",
    )(v)


def _final_kernel(ht_ref, ft_ref, g_ref, y_ref):
    h = ht_ref[...] + ft_ref[...]
    ms = jnp.mean(h * h, axis=0, keepdims=True)
    y = (h * lax.rsqrt(ms + EPS)) * g_ref[...]
    y_ref[...] = y.T


def _final(ht, ft, g_col, tm):
    t = ht.shape[1]
    return pl.pallas_call(
        _final_kernel,
        grid=(t // tm,),
        in_specs=[
            pl.BlockSpec((D_MODEL, tm), lambda i: (0, i)),
            pl.BlockSpec((D_MODEL, tm), lambda i: (0, i)),
            pl.BlockSpec((D_MODEL, 1), lambda i: (0, 0)),
        ],
        out_specs=pl.BlockSpec((tm, D_MODEL), lambda i: (i, 0)),
        out_shape=jax.ShapeDtypeStruct((t, D_MODEL), F32),
        compiler_params=_cparams(("arbitrary",)),
        name="final",
    )(ht, ft, g_col)


def kernel(x_prompt, x_sample, state_a_k, state_a_v, state_b_k, state_b_v, g_attn, w_in, attn_sink,
           g_out_a, g_out_b, w_out, g_ffn, peer_wq, peer_k1, peer_k2, peer_u, peer_v, g_final):
    bsz, seq, _ = x_prompt.shape
    db, dt, _ = x_sample.shape
    assert w_in.shape[0] == 1 and dt == 1
    past = PAST_LEN
    lb = state_b_k.shape[2]
    assert lb == max(w for w, _ in DILATED)

    w_in_b = w_in[0].astype(BF16)
    w_out_b = w_out[0].astype(BF16)
    wq_t = peer_wq[0].reshape(D_MODEL, PEER_HEADS * PEER_DKEY).T.astype(BF16)
    k1_b = peer_k1[0].astype(BF16)
    k2_b = peer_k2[0].astype(BF16)
    u_b = peer_u[0].astype(BF16)
    vt_b = _cast_transposed(peer_v[0])

    rope_flags = jnp.concatenate([
        jnp.ones((1, WIDTH_A + N_KV_A * HEAD_DIM), F32), jnp.zeros((1, N_KV_A * HEAD_DIM), F32),
        jnp.ones((1, 2 * WIDTH_B), F32), jnp.zeros((1, WIDTH_B), F32)], axis=-1)
    cos_p, sin_p = _rope_tables(jnp.arange(seq))
    cos_s, sin_s = _rope_tables(jnp.full((db,), past))

    tm_proj, tm_merge = 1024, 512
    gf_col = g_ffn.reshape(D_MODEL, 1)
    xp = x_prompt.reshape(bsz * seq, D_MODEL)
    pp = _proj(xp, g_attn, w_in_b, cos_p, sin_p, rope_flags, tm_proj, seq // tm_proj)
    pp4 = pp.reshape(QKV_COLS // LANES, bsz, seq, LANES)
    oa_p = _attn_a(pp4, attn_sink[0])
    ob_p = _attn_b(pp4)
    ht_p, xt_p = _merge(oa_p.reshape(bsz * seq, WIDTH_A), ob_p.reshape(bsz * seq, WIDTH_B), xp,
                        g_out_a, g_out_b, w_out_b, gf_col, tm_merge)

    xs = x_sample.reshape(db, D_MODEL)
    ps = _proj(xs, g_attn, w_in_b, cos_s, sin_s, rope_flags, db, 1)

    def sample_cols(lo, hi):
        return jnp.transpose(ps[lo:hi], (1, 0, 2)).reshape(db, (hi - lo) * LANES)

    qa_s = sample_cols(COL_QA, COL_KA).reshape(db, N_HEADS_A, HEAD_DIM)
    ka_s = sample_cols(COL_KA, COL_VA).reshape(db, N_KV_A, HEAD_DIM)
    va_s = sample_cols(COL_VA, COL_QB).reshape(db, N_KV_A, HEAD_DIM)
    kb_s = sample_cols(COL_KB, COL_VB)
    vb_s = sample_cols(COL_VB, QKV_COLS // LANES)
    kt_a = jnp.transpose(state_a_k[0], (0, 2, 3, 1))
    vt_a = jnp.transpose(state_a_v[0], (0, 2, 3, 1))
    skt_b = jnp.transpose(state_b_k[0], (0, 2, 3, 1))
    svt_b = jnp.transpose(state_b_v[0], (0, 2, 3, 1))
    oa_s = _samp_a(qa_s, kt_a, vt_a, ka_s, va_s, attn_sink[0].reshape(N_KV_A, G_A, 1))
    obt_s = _samp_b(skt_b, svt_b, sample_cols(COL_QB, COL_KB).T, kb_s.T, vb_s.T)
    ht_s, xt_s = _merge(oa_s.reshape(db, WIDTH_A), obt_s.T, xs, g_out_a, g_out_b, w_out_b, gf_col, db)

    g_col = g_final.reshape(D_MODEL, 1)

    def peer_and_final(ht, xt, tm_router, tm_experts, tm_final):
        th, e1, s2, e2 = _router(xt, wq_t, k1_b, k2_b, tm_router)
        ft = _experts(xt, u_b, vt_b, th, e1, s2, e2, tm_experts)
        return _final(ht, ft, g_col, tm_final)

    y_p = peer_and_final(ht_p, xt_p, 1024, 512, 256).reshape(bsz, seq, D_MODEL)
    y_s = peer_and_final(ht_s, xt_s, db, db, db).reshape(db, 1, D_MODEL)

    def prompt_cols(lo, hi, first_row, heads):
        blk = jnp.transpose(pp4[lo:hi, :, first_row:], (1, 2, 0, 3))
        return blk.reshape(1, bsz, seq - first_row, heads, HEAD_DIM)

    rows_a = min(WINDOW_A, seq)
    ka_p = prompt_cols(COL_KA, COL_VA, seq - rows_a, N_KV_A)
    va_p = prompt_cols(COL_VA, COL_QB, seq - rows_a, N_KV_A)
    kb_p = prompt_cols(COL_KB, COL_VB, 0, N_HEADS_B)
    vb_p = prompt_cols(COL_VB, QKV_COLS // LANES, 0, N_HEADS_B)
    return (y_p, y_s, ka_p, va_p, kb_p, vb_p,
            ka_s.reshape(1, db, 1, N_KV_A, HEAD_DIM), va_s.reshape(1, db, 1, N_KV_A, HEAD_DIM),
            kb_s.reshape(1, db, 1, N_HEADS_B, HEAD_DIM), vb_s.reshape(1, db, 1, N_HEADS_B, HEAD_DIM))
```

```python
import functools
import math

import jax
import jax.numpy as jnp
import numpy as np
from jax import lax
from jax.experimental import pallas as pl
from jax.experimental.pallas import tpu as pltpu

D_MODEL = 4096
HEAD_DIM = 64
WIDTH_A = 2048
WIDTH_B = 2048
N_HEADS_A = 32
N_KV_A = 4
G_A = 8
N_HEADS_B = 32
WINDOW_A = 128
DILATED = ((128, 1), (512, 4), (2048, 16))
BLOCK = 128
PAST_LEN = 8192
ROPE_THETA = 10000.0
SCALE = HEAD_DIM ** -0.5
PEER_HEADS = 8
PEER_NKEYS = 128
PEER_EXPERTS = PEER_NKEYS * PEER_NKEYS
PEER_DKEY = 256
PEER_TOPK = 16
EPS = 1e-6
QKV_COLS = WIDTH_A + 2 * N_KV_A * HEAD_DIM + 3 * WIDTH_B

COL_QA = 0
COL_KA = WIDTH_A // 128
COL_VA = COL_KA + N_KV_A * HEAD_DIM // 128
COL_QB = COL_VA + N_KV_A * HEAD_DIM // 128
COL_KB = COL_QB + WIDTH_B // 128
COL_VB = COL_KB + WIDTH_B // 128

LANES = 128
VMEM_LIMIT = 60 * 1024 * 1024

BF16 = jnp.bfloat16
F32 = jnp.float32
NEG_INF = float("-inf")


def _cparams(sem):
    return pltpu.CompilerParams(dimension_semantics=sem, vmem_limit_bytes=VMEM_LIMIT)


PROJ_TN = 512


def _proj_kernel(x_ref, g_ref, w_ref, cos_ref, sin_ref, rope_ref, o_ref, xn_ref):
    j = pl.program_id(1)

    @pl.when(j == 0)
    def _():
        x = x_ref[...]
        ms = jnp.mean(x * x, axis=-1, keepdims=True)
        xn_ref[...] = ((x * lax.rsqrt(ms + EPS)) * g_ref[...]).astype(BF16)

    p = jnp.dot(xn_ref[...], w_ref[...], preferred_element_type=F32)

    lane = lax.broadcasted_iota(jnp.int32, (1, LANES), 1)
    first_half = (lane % HEAD_DIM) < (HEAD_DIM // 2)
    cos = cos_ref[...]
    sin = sin_ref[...]
    for c in range(PROJ_TN // LANES):
        sl = slice(c * LANES, (c + 1) * LANES)
        pc = p[:, sl]
        partner = jnp.where(first_half, pltpu.roll(pc, LANES - HEAD_DIM // 2, 1),
                            pltpu.roll(pc, HEAD_DIM // 2, 1))
        roped = pc * cos + partner * sin
        o_ref[c] = jnp.where(rope_ref[:, sl] > 0.0, roped, pc)


def _proj(x2d, g, w_bf, cos_t, sin_t, rope_flags, tm, pos_blocks):
    t = x2d.shape[0]
    grid = (t // tm, QKV_COLS // PROJ_TN)
    return pl.pallas_call(
        _proj_kernel,
        grid=grid,
        in_specs=[
            pl.BlockSpec((tm, D_MODEL), lambda i, j: (i, 0), pipeline_mode=pl.Buffered(1)),
            pl.BlockSpec((1, D_MODEL), lambda i, j: (0, 0)),
            pl.BlockSpec((D_MODEL, PROJ_TN), lambda i, j: (0, j)),
            pl.BlockSpec((tm, LANES), lambda i, j: (i % pos_blocks, 0)),
            pl.BlockSpec((tm, LANES), lambda i, j: (i % pos_blocks, 0)),
            pl.BlockSpec((1, PROJ_TN), lambda i, j: (0, j)),
        ],
        out_specs=pl.BlockSpec((PROJ_TN // LANES, tm, LANES), lambda i, j: (j, i, 0)),
        out_shape=jax.ShapeDtypeStruct((QKV_COLS // LANES, t, LANES), F32),
        scratch_shapes=[pltpu.VMEM((tm, D_MODEL), BF16)],
        compiler_params=_cparams(("arbitrary", "arbitrary")),
        name="proj",
    )(x2d, g, w_bf, cos_t, sin_t, rope_flags)


def _rope_tables(pos):
    half = HEAD_DIM // 2
    inv = ROPE_THETA ** (-jnp.arange(half, dtype=F32) / half)
    ang = pos.astype(F32)[:, None] * inv[None, :]
    cos = jnp.cos(ang)
    sin = jnp.sin(ang)
    cos_t = jnp.concatenate([cos, cos, cos, cos], axis=-1)
    sin_t = jnp.concatenate([-sin, sin, -sin, sin], axis=-1)
    return cos_t, sin_t


def _attn_a_kernel(sink_ref, q_ref, kp_ref, kc_ref, vp_ref, vc_ref, o_ref):
    kp_id = pl.program_id(1)
    n = pl.program_id(2)
    lane = lax.broadcasted_iota(jnp.int32, (1, LANES), 1)
    lo = lane < HEAD_DIM
    qi = lax.broadcasted_iota(jnp.int32, (BLOCK, 2 * BLOCK), 0)
    kj = lax.broadcasted_iota(jnp.int32, (BLOCK, 2 * BLOCK), 1)
    first_key = jnp.where(n == 0, BLOCK, 0)
    valid = (kj >= qi) & (kj <= qi + WINDOW_A) & (kj >= first_key)
    k2 = jnp.concatenate([kp_ref[...], kc_ref[...]], axis=0)
    v2 = jnp.concatenate([vp_ref[...], vc_ref[...]], axis=0)
    nt = (((1,), (1,)), ((), ()))
    for kvl in range(2):
        if kvl == 0:
            k_lo = jnp.where(lo, k2, 0.0)
            k_hi = pltpu.roll(k_lo, HEAD_DIM, 1)
            v_lo = jnp.where(lo, v2, 0.0)
            v_hi = pltpu.roll(v_lo, HEAD_DIM, 1)
        else:
            k_hi = jnp.where(lo, 0.0, k2)
            k_lo = pltpu.roll(k_hi, HEAD_DIM, 1)
            v_hi = jnp.where(lo, 0.0, v2)
            v_lo = pltpu.roll(v_hi, HEAD_DIM, 1)
        kb = (k_lo.astype(BF16), k_hi.astype(BF16))
        vb = (v_lo.astype(BF16), v_hi.astype(BF16))
        scores = []
        for c in range(G_A // 2):
            qc = q_ref[kvl * (G_A // 2) + c].astype(BF16)
            for par in range(2):
                scores.append(lax.dot_general(qc, kb[par], nt, preferred_element_type=F32) * SCALE)
        probs = []
        for i, s in enumerate(scores):
            sk = sink_ref[kp_id * (2 * G_A) + kvl * G_A + i]
            s = jnp.where(valid, s, NEG_INF)
            m = jnp.maximum(jnp.max(s, axis=-1, keepdims=True), sk)
            p = jnp.exp(s - m)
            den = jnp.sum(p, axis=-1, keepdims=True) + jnp.exp(sk - m)
            probs.append((p.astype(BF16), 1.0 / den))
        for c in range(G_A // 2):
            col = (kvl * (G_A // 2) + c) * LANES
            acc = jnp.zeros((BLOCK, LANES), F32)
            for par in range(2):
                p, rden = probs[2 * c + par]
                acc = acc + jnp.dot(p, vb[par], preferred_element_type=F32) * rden
            o_ref[:, col:col + LANES] = acc


def _attn_a(p4, sink):
    _, b, s, _ = p4.shape
    nb = s // BLOCK
    qb = 2 * G_A * HEAD_DIM // LANES
    grid = (b, N_KV_A // 2, nb)
    prev = lambda n: jnp.maximum(n - 1, 0)
    return pl.pallas_call(
        _attn_a_kernel,
        grid=grid,
        in_specs=[
            pl.BlockSpec(memory_space=pltpu.SMEM),
            pl.BlockSpec((qb, None, BLOCK, LANES), lambda bi, kp, n: (kp, bi, n, 0)),
            pl.BlockSpec((None, None, BLOCK, LANES), lambda bi, kp, n: (COL_KA + kp, bi, prev(n), 0)),
            pl.BlockSpec((None, None, BLOCK, LANES), lambda bi, kp, n: (COL_KA + kp, bi, n, 0)),
            pl.BlockSpec((None, None, BLOCK, LANES), lambda bi, kp, n: (COL_VA + kp, bi, prev(n), 0)),
            pl.BlockSpec((None, None, BLOCK, LANES), lambda bi, kp, n: (COL_VA + kp, bi, n, 0)),
        ],
        out_specs=pl.BlockSpec((None, BLOCK, qb * LANES), lambda bi, kp, n: (bi, n, kp)),
        out_shape=jax.ShapeDtypeStruct((b, s, WIDTH_A), F32),
        compiler_params=_cparams(("arbitrary", "arbitrary", "arbitrary")),
        name="attn_a",
    )(sink, p4, p4, p4, p4, p4)


def _attn_b_kernel(q_ref, k_ref, v_ref, o_ref, oc_ref, lc_ref):
    seq = q_ref.shape[0]
    lane = lax.broadcasted_iota(jnp.int32, (1, LANES), 1)
    lo = lane < HEAD_DIM
    nt = (((1,), (1,)), ((), ()))

    masks = (lo, jnp.logical_not(lo))

    def blocks(cfg, d, starts, has_prev):
        nk = 2 * BLOCK if has_prev else BLOCK
        qi = lax.broadcasted_iota(jnp.int32, (BLOCK, nk), 0)
        kj = lax.broadcasted_iota(jnp.int32, (BLOCK, nk), 1)
        if has_prev:
            valid = (kj >= qi) & (kj <= qi + BLOCK)
        else:
            valid = kj <= qi
        rows_qs, vs, scores = [], [], []
        for start in starts:
            rows_q = pl.ds(start, BLOCK, stride=d) if d > 1 else pl.ds(start, BLOCK)
            kstart = start - BLOCK * d if has_prev else start
            rows_k = pl.ds(kstart, nk, stride=d) if d > 1 else pl.ds(kstart, nk)
            q = q_ref[rows_q, :].astype(BF16)
            k = k_ref[rows_k, :]
            rows_qs.append(rows_q)
            vs.append(v_ref[rows_k, :])
            for msk in masks:
                kpar = jnp.where(msk, k, 0.0).astype(BF16)
                scores.append(lax.dot_general(q, kpar, nt, preferred_element_type=F32) * SCALE)
        probs = []
        for s in scores:
            s = jnp.where(valid, s, NEG_INF)
            m = jnp.max(s, axis=-1, keepdims=True)
            p = jnp.exp(s - m)
            den = jnp.sum(p, axis=-1, keepdims=True)
            probs.append((p.astype(BF16), 1.0 / den, jnp.log(den) + m))
        for b, (rows_q, v) in enumerate(zip(rows_qs, vs)):
            acc = jnp.zeros((BLOCK, LANES), F32)
            lse = jnp.zeros((BLOCK, LANES), F32)
            for par, msk in enumerate(masks):
                p, rden, l = probs[2 * b + par]
                vpar = jnp.where(msk, v, 0.0).astype(BF16)
                acc = acc + jnp.dot(p, vpar, preferred_element_type=F32) * rden
                lse = jnp.where(msk, l, lse)
            oc_ref[cfg, rows_q, :] = acc
            lc_ref[cfg, rows_q, :] = lse

    def block(cfg, d, start, has_prev):
        blocks(cfg, d, [start], has_prev)

    for cfg, (w, d) in enumerate(DILATED):
        assert w // d == BLOCK
        nblk = seq // d // BLOCK
        span = BLOCK * d
        if d == 1:
            block(cfg, d, 0, False)
            cnt = nblk - 1
            group = next(gs for gs in (5, 4, 3, 2, 1) if cnt % gs == 0)

            def body_seq(it, carry, cfg=cfg, d=d, span=span, group=group):
                blocks(cfg, d, [(1 + it * group + u) * span for u in range(group)], True)
                return carry

            lax.fori_loop(0, cnt // group, body_seq, 0)
            continue
        cg = min(d, 4)
        ncg = d // cg

        def body_first(it, carry, cfg=cfg, d=d, cg=cg):
            blocks(cfg, d, [it * cg + u for u in range(cg)], False)
            return carry

        if ncg == 1:
            body_first(0, 0)
        else:
            lax.fori_loop(0, ncg, body_first, 0)
        if nblk > 1:
            def body_rest(it, carry, cfg=cfg, d=d, cg=cg, ncg=ncg, span=span):
                n = 1 + it // ncg
                rc = it % ncg
                blocks(cfg, d, [rc * cg + u + n * span for u in range(cg)], True)
                return carry

            lax.fori_loop(0, (nblk - 1) * ncg, body_rest, 0)

    def combine(i, carry):
        rows = pl.ds(pl.multiple_of(i * BLOCK, BLOCK), BLOCK)
        l0 = lc_ref[0, rows, :]
        l1 = lc_ref[1, rows, :]
        l2 = lc_ref[2, rows, :]
        mx = jnp.maximum(jnp.maximum(l0, l1), l2)
        w0 = jnp.exp(l0 - mx)
        w1 = jnp.exp(l1 - mx)
        w2 = jnp.exp(l2 - mx)
        num = w0 * oc_ref[0, rows, :] + w1 * oc_ref[1, rows, :] + w2 * oc_ref[2, rows, :]
        o_ref[rows, :] = num / (w0 + w1 + w2)
        return carry

    lax.fori_loop(0, seq // BLOCK, combine, 0)


def _attn_b(p4):
    _, b, s, _ = p4.shape
    grid = (b, N_HEADS_B // 2)
    ncfg = len(DILATED)
    return pl.pallas_call(
        _attn_b_kernel,
        grid=grid,
        in_specs=[
            pl.BlockSpec((None, None, s, LANES), lambda bi, hp: (COL_QB + hp, bi, 0, 0)),
            pl.BlockSpec((None, None, s, LANES), lambda bi, hp: (COL_KB + hp, bi, 0, 0)),
            pl.BlockSpec((None, None, s, LANES), lambda bi, hp: (COL_VB + hp, bi, 0, 0)),
        ],
        out_specs=pl.BlockSpec((None, s, LANES), lambda bi, hp: (bi, 0, hp)),
        out_shape=jax.ShapeDtypeStruct((b, s, WIDTH_B), F32),
        scratch_shapes=[pltpu.VMEM((ncfg, s, LANES), F32), pltpu.VMEM((ncfg, s, LANES), F32)],
        compiler_params=_cparams(("arbitrary", "arbitrary")),
        name="attn_b",
    )(p4, p4, p4)


SAMP_A_BB = 8


def _samp_a_kernel(q_ref, kt_ref, vt_ref, kn_ref, vn_ref, sink_ref, o_ref):
    nt = (((1,), (1,)), ((), ()))
    for bb in range(SAMP_A_BB):
        for kv in range(N_KV_A):
            q = q_ref[bb, kv * G_A:(kv + 1) * G_A, :]
            kt = kt_ref[bb, kv]
            vt = vt_ref[bb, kv]
            kn = kn_ref[bb, kv:kv + 1, :]
            vn = vn_ref[bb, kv:kv + 1, :]
            sk = sink_ref[kv]
            s = jnp.dot(q.astype(BF16), kt.astype(BF16), preferred_element_type=F32) * SCALE
            sn = jnp.sum(q * kn, axis=-1, keepdims=True) * SCALE
            m = jnp.maximum(jnp.maximum(jnp.max(s, axis=-1, keepdims=True), sn), sk)
            p = jnp.exp(s - m)
            pn = jnp.exp(sn - m)
            den = jnp.sum(p, axis=-1, keepdims=True) + pn + jnp.exp(sk - m)
            o = lax.dot_general(p.astype(BF16), vt.astype(BF16), nt, preferred_element_type=F32)
            o_ref[bb, kv * G_A:(kv + 1) * G_A, :] = (o + pn * vn) / den


def _samp_a(qa_s, kt_a, vt_a, kn, vn, sink3):
    db = qa_s.shape[0]
    lb = kt_a.shape[-1]
    assert lb <= WINDOW_A
    bb = SAMP_A_BB
    return pl.pallas_call(
        _samp_a_kernel,
        grid=(db // bb,),
        in_specs=[
            pl.BlockSpec((bb, N_HEADS_A, HEAD_DIM), lambda b: (b, 0, 0)),
            pl.BlockSpec((bb, N_KV_A, HEAD_DIM, lb), lambda b: (b, 0, 0, 0)),
            pl.BlockSpec((bb, N_KV_A, HEAD_DIM, lb), lambda b: (b, 0, 0, 0)),
            pl.BlockSpec((bb, N_KV_A, HEAD_DIM), lambda b: (b, 0, 0)),
            pl.BlockSpec((bb, N_KV_A, HEAD_DIM), lambda b: (b, 0, 0)),
            pl.BlockSpec((N_KV_A, G_A, 1), lambda b: (0, 0, 0)),
        ],
        out_specs=pl.BlockSpec((bb, N_HEADS_A, HEAD_DIM), lambda b: (b, 0, 0)),
        out_shape=jax.ShapeDtypeStruct((db, N_HEADS_A, HEAD_DIM), F32),
        compiler_params=_cparams(("arbitrary",)),
        name="samp_a",
    )(qa_s, kt_a, vt_a, kn, vn, sink3)


SAMP_B_HG = 16


def _samp_b_kernel(kt_ref, vt_ref, qt_ref, knt_ref, vnt_ref, o_ref):
    b = pl.program_id(1)
    lb = kt_ref.shape[-1]
    laneb = lax.broadcasted_iota(jnp.int32, (1, LANES), 1) == b

    def column(ref):
        return jnp.sum(jnp.where(laneb, ref[...], 0.0), axis=-1, keepdims=True)

    qcol = column(qt_ref)
    kncol = column(knt_ref)
    vncol = column(vnt_ref)
    hrow = lax.broadcasted_iota(jnp.int32, (SAMP_B_HG, 1), 0)
    s = jnp.zeros((SAMP_B_HG, lb), F32)
    sn = jnp.zeros((SAMP_B_HG, 1), F32)
    for h in range(SAMP_B_HG):
        hs = slice(h * HEAD_DIM, (h + 1) * HEAD_DIM)
        s_h = jnp.sum(kt_ref[0, h] * qcol[hs], axis=0, keepdims=True)
        sn_h = jnp.sum(qcol[hs] * kncol[hs], axis=0, keepdims=True)
        s = jnp.where(hrow == h, s_h, s)
        sn = jnp.where(hrow == h, sn_h, sn)
    s = s * SCALE
    sn = sn * SCALE
    dist = lb - lax.broadcasted_iota(jnp.int32, (1, lb), 1)
    ps, pns, lses = [], [], []
    for w, d in DILATED:
        valid = ((dist % d) == 0) & (dist <= w)
        m = jnp.maximum(jnp.max(jnp.where(valid, s, NEG_INF), axis=-1, keepdims=True), sn)
        p = jnp.where(valid, jnp.exp(s - m), 0.0)
        pn = jnp.exp(sn - m)
        den = jnp.sum(p, axis=-1, keepdims=True) + pn
        ps.append(p / den)
        pns.append(pn / den)
        lses.append(jnp.log(den) + m)
    mx = functools.reduce(jnp.maximum, lses)
    ws = [jnp.exp(l - mx) for l in lses]
    wsum = functools.reduce(lambda a, c: a + c, ws)
    pmix = functools.reduce(lambda a, c: a + c, [w_ * p_ for w_, p_ in zip(ws, ps)]) / wsum
    pnmix = functools.reduce(lambda a, c: a + c, [w_ * p_ for w_, p_ in zip(ws, pns)]) / wsum
    cols = []
    for h in range(SAMP_B_HG):
        hs = slice(h * HEAD_DIM, (h + 1) * HEAD_DIM)
        oc = jnp.sum(vt_ref[0, h] * pmix[h:h + 1, :], axis=-1, keepdims=True)
        cols.append(oc + pnmix[h:h + 1, :] * vncol[hs])
    ocol = jnp.concatenate(cols, axis=0)

    @pl.when(b == 0)
    def _():
        o_ref[...] = jnp.zeros_like(o_ref)

    o_ref[...] = jnp.where(laneb, ocol, o_ref[...])


def _samp_b(kt_b, vt_b, qt, knt, vnt):
    db, nh, hd, lb = kt_b.shape
    assert db == LANES
    rows = SAMP_B_HG * HEAD_DIM
    grid = (nh // SAMP_B_HG, db)
    return pl.pallas_call(
        _samp_b_kernel,
        grid=grid,
        in_specs=[
            pl.BlockSpec((1, SAMP_B_HG, hd, lb), lambda g, b: (b, g, 0, 0)),
            pl.BlockSpec((1, SAMP_B_HG, hd, lb), lambda g, b: (b, g, 0, 0)),
            pl.BlockSpec((rows, db), lambda g, b: (g, 0)),
            pl.BlockSpec((rows, db), lambda g, b: (g, 0)),
            pl.BlockSpec((rows, db), lambda g, b: (g, 0)),
        ],
        out_specs=pl.BlockSpec((rows, db), lambda g, b: (g, 0)),
        out_shape=jax.ShapeDtypeStruct((nh * hd, db), F32),
        compiler_params=_cparams(("arbitrary", "arbitrary")),
        name="samp_b",
    )(kt_b, vt_b, qt, knt, vnt)


MERGE_TN = 1024
MERGE_NORM_ROWS = 512


def _merge_kernel(oa_ref, ob_ref, x_ref, ga_ref, gb_ref, w_ref, gfc_ref, ht_ref, xt_ref, cat_ref, ssq_ref):
    j = pl.program_id(1)
    nj = pl.num_programs(1)

    @pl.when(j == 0)
    def _():
        oa = oa_ref[...]
        ob = ob_ref[...]
        ya = (oa * lax.rsqrt(jnp.mean(oa * oa, axis=-1, keepdims=True) + EPS)) * ga_ref[...]
        yb = (ob * lax.rsqrt(jnp.mean(ob * ob, axis=-1, keepdims=True) + EPS)) * gb_ref[...]
        cat_ref[:, :WIDTH_A] = ya.astype(BF16)
        cat_ref[:, WIDTH_A:] = yb.astype(BF16)
        ssq_ref[...] = jnp.zeros_like(ssq_ref)

    h_t = (x_ref[...] + jnp.dot(cat_ref[...], w_ref[...], preferred_element_type=F32)).T
    ht_ref[pl.ds(pl.multiple_of(j * MERGE_TN, MERGE_TN), MERGE_TN), :] = h_t
    ssq_ref[...] += jnp.sum(h_t * h_t, axis=0, keepdims=True)

    @pl.when(j == nj - 1)
    def _():
        rinv = lax.rsqrt(ssq_ref[...] * (1.0 / D_MODEL) + EPS)
        for c in range(D_MODEL // MERGE_NORM_ROWS):
            rs = slice(c * MERGE_NORM_ROWS, (c + 1) * MERGE_NORM_ROWS)
            xt_ref[rs, :] = ((ht_ref[rs, :] * rinv) * gfc_ref[rs, :]).astype(BF16)


def _merge(oa, ob, x2d, ga, gb, w_bf, gf_col, tm):
    t = x2d.shape[0]
    nj = D_MODEL // MERGE_TN
    grid = (t // tm, nj)
    once = pl.Buffered(1)
    return pl.pallas_call(
        _merge_kernel,
        grid=grid,
        in_specs=[
            pl.BlockSpec((tm, WIDTH_A), lambda i, j: (i, 0), pipeline_mode=once),
            pl.BlockSpec((tm, WIDTH_B), lambda i, j: (i, 0), pipeline_mode=once),
            pl.BlockSpec((tm, MERGE_TN), lambda i, j: (i, j)),
            pl.BlockSpec((1, WIDTH_A), lambda i, j: (0, 0)),
            pl.BlockSpec((1, WIDTH_B), lambda i, j: (0, 0)),
            pl.BlockSpec((D_MODEL, MERGE_TN), lambda i, j: (0, j)),
            pl.BlockSpec((D_MODEL, 1), lambda i, j: (0, 0), pipeline_mode=once),
        ],
        out_specs=[
            pl.BlockSpec((D_MODEL, tm), lambda i, j: (0, i), pipeline_mode=once),
            pl.BlockSpec((D_MODEL, tm), lambda i, j: (0, i), pipeline_mode=once),
        ],
        out_shape=[
            jax.ShapeDtypeStruct((D_MODEL, t), F32),
            jax.ShapeDtypeStruct((D_MODEL, t), BF16),
        ],
        scratch_shapes=[pltpu.VMEM((tm, D_MODEL), BF16), pltpu.VMEM((1, tm), F32)],
        compiler_params=_cparams(("arbitrary", "arbitrary")),
        name="merge",
    )(oa, ob, x2d, ga, gb, w_bf, gf_col)


GATE_ROWS = 4


def _top_values(x, k):
    row = lax.broadcasted_iota(jnp.int32, (k, x.shape[1]), 0)
    vals = jnp.zeros((k, x.shape[1]), F32)
    cur = x
    for r in range(k):
        m = jnp.max(cur, axis=0, keepdims=True)
        vals = jnp.where(row == r, m, vals)
        if r + 1 < k:
            cur = jnp.where(cur == m, NEG_INF, cur)
    return vals


def _router_kernel(x_ref, wq_ref, k1_ref, k2_ref, th_ref, e1_ref, s2_ref, e2_ref):
    half = PEER_DKEY // 2
    qt = jnp.dot(wq_ref[...], x_ref[...], preferred_element_type=F32)
    s1_all = jnp.dot(k1_ref[0], qt[:half].astype(BF16), preferred_element_type=F32)
    s2_all = jnp.dot(k2_ref[0], qt[half:].astype(BF16), preferred_element_type=F32)
    for c in range(x_ref.shape[1] // LANES):
        sl = slice(c * LANES, (c + 1) * LANES)
        s1 = s1_all[:, sl]
        s2 = s2_all[:, sl]
        v1 = _top_values(s1, PEER_TOPK)
        v2 = _top_values(s2, PEER_TOPK)
        cands = [v1[0:1] + v2[0:8], v1[0:1] + v2[8:16]]
        cands += [v1[a:a + 1] + v2[0:8] for a in range(1, 8)]
        cands += [v1[8:16] + v2[0:1]]
        cand = jnp.concatenate(cands, axis=0)
        tk = _top_values(cand, PEER_TOPK)[PEER_TOPK - 1:PEER_TOPK]
        m1 = v1[0:1]
        m2 = v2[0:1]
        z = jnp.sum(jnp.where(cand >= tk, jnp.exp(cand - (m1 + m2)), 0.0), axis=0, keepdims=True)
        theta = jnp.full(s1.shape, jnp.inf, F32)
        for b in range(PEER_TOPK):
            vb = v2[b:b + 1]
            theta = jnp.where((s1 + vb) >= tk, vb, theta)
        theta = jnp.where(s1 >= v1[PEER_TOPK - 1:PEER_TOPK], theta, jnp.inf)
        e1 = jnp.exp(s1 - m1) / z
        for grp in range(PEER_NKEYS // GATE_ROWS):
            rows = slice(grp * GATE_ROWS, (grp + 1) * GATE_ROWS)
            th_ref[0, grp, :, sl] = theta[rows]
            e1_ref[0, grp, :, sl] = e1[rows]
        s2_ref[0, :, sl] = s2
        e2_ref[0, :, sl] = jnp.exp(s2 - m2)


def _router(xt, wqt, k1b, k2b, tm):
    t = xt.shape[1]
    grid = (t // tm, PEER_HEADS)
    out = jax.ShapeDtypeStruct((PEER_HEADS, PEER_NKEYS, t), F32)
    ospec = pl.BlockSpec((1, PEER_NKEYS, tm), lambda i, h: (h, 0, i))
    ngrp = PEER_NKEYS // GATE_ROWS
    gout = jax.ShapeDtypeStruct((PEER_HEADS, ngrp, GATE_ROWS, t), F32)
    gspec = pl.BlockSpec((1, ngrp, GATE_ROWS, tm), lambda i, h: (h, 0, 0, i))
    return pl.pallas_call(
        _router_kernel,
        grid=grid,
        in_specs=[
            pl.BlockSpec((D_MODEL, tm), lambda i, h: (0, i)),
            pl.BlockSpec((PEER_DKEY, D_MODEL), lambda i, h: (h, 0)),
            pl.BlockSpec((1, PEER_NKEYS, PEER_DKEY // 2), lambda i, h: (h, 0, 0)),
            pl.BlockSpec((1, PEER_NKEYS, PEER_DKEY // 2), lambda i, h: (h, 0, 0)),
        ],
        out_specs=[gspec, gspec, ospec, ospec],
        out_shape=[gout, gout, out, out],
        compiler_params=_cparams(("arbitrary", "arbitrary")),
        name="router",
    )(xt, wqt, k1b, k2b)


EXP_EB = 512
EXP_IB = EXP_EB // PEER_NKEYS
assert EXP_IB == GATE_ROWS


def _experts_kernel(x_ref, u_ref, vt_ref, th_ref, e1_ref, s2_ref, e2_ref, o_ref, at_ref):
    e = pl.program_id(1)

    @pl.when(e == 0)
    def _():
        o_ref[...] = jnp.zeros_like(o_ref)

    ht = jnp.dot(u_ref[...], x_ref[...], preferred_element_type=F32)
    for il in range(EXP_IB):
        rs = slice(il * PEER_NKEYS, (il + 1) * PEER_NKEYS)
        for c in range(x_ref.shape[1] // LANES):
            sl = slice(c * LANES, (c + 1) * LANES)
            g = jnp.zeros((PEER_NKEYS, LANES), F32)
            for h in range(PEER_HEADS):
                th = th_ref[h, 0, il:il + 1, sl]
                e1 = e1_ref[h, 0, il:il + 1, sl]
                g = g + jnp.where(s2_ref[h, :, sl] >= th, e2_ref[h, :, sl], 0.0) * e1
            at_ref[rs, sl] = (jax.nn.gelu(ht[rs, sl]) * g).astype(BF16)
    o_ref[...] += jnp.dot(vt_ref[...], at_ref[...], preferred_element_type=F32)


def _experts(xt, u_bf, vt_bf, th4, e14, s2, e2, tm):
    t = xt.shape[1]
    grid = (t // tm, PEER_EXPERTS // EXP_EB)
    return pl.pallas_call(
        _experts_kernel,
        grid=grid,
        in_specs=[
            pl.BlockSpec((D_MODEL, tm), lambda i, e: (0, i)),
            pl.BlockSpec((EXP_EB, D_MODEL), lambda i, e: (e, 0)),
            pl.BlockSpec((D_MODEL, EXP_EB), lambda i, e: (0, e)),
            pl.BlockSpec((PEER_HEADS, 1, EXP_IB, tm), lambda i, e: (0, e, 0, i)),
            pl.BlockSpec((PEER_HEADS, 1, EXP_IB, tm), lambda i, e: (0, e, 0, i)),
            pl.BlockSpec((PEER_HEADS, PEER_NKEYS, tm), lambda i, e: (0, 0, i)),
            pl.BlockSpec((PEER_HEADS, PEER_NKEYS, tm), lambda i, e: (0, 0, i)),
        ],
        out_specs=pl.BlockSpec((D_MODEL, tm), lambda i, e: (0, i)),
        out_shape=jax.ShapeDtypeStruct((D_MODEL, t), F32),
        scratch_shapes=[pltpu.VMEM((EXP_EB, tm), BF16)],
        compiler_params=_cparams(("arbitrary", "arbitrary")),
        name="experts",
    )(xt, u_bf, vt_bf, th4, e14, s2, e2)


CAST_ROWS = 1024
CAST_COLS = 2048


def _cast_t_kernel(v_ref, o_ref):
    o_ref[...] = v_ref[...].T.astype(BF16)


def _cast_transposed(v):
    e, d = v.shape
    return pl.pallas_call(
        _cast_t_kernel,
        grid=(e // CAST_ROWS, d // CAST_COLS),
        in_specs=[pl.BlockSpec((CAST_ROWS, CAST_COLS), lambda i, j: (i, j))],
        out_specs=pl.BlockSpec((CAST_COLS, CAST_ROWS), lambda i, j: (j, i)),
        out_shape=jax.ShapeDtypeStruct((d, e), BF16),
        compiler_params=_cparams(("arbitrary", "arbitrary")),
        name="cast_t",
    )(v)


def _final_kernel(ht_ref, ft_ref, g_ref, y_ref):
    h = ht_ref[...] + ft_ref[...]
    ms = jnp.mean(h * h, axis=0, keepdims=True)
    y = (h * lax.rsqrt(ms + EPS)) * g_ref[...]
    y_ref[...] = y.T


def _final(ht, ft, g_col, tm):
    t = ht.shape[1]
    return pl.pallas_call(
        _final_kernel,
        grid=(t // tm,),
        in_specs=[
            pl.BlockSpec((D_MODEL, tm), lambda i: (0, i)),
            pl.BlockSpec((D_MODEL, tm), lambda i: (0, i)),
            pl.BlockSpec((D_MODEL, 1), lambda i: (0, 0)),
        ],
        out_specs=pl.BlockSpec((tm, D_MODEL), lambda i: (i, 0)),
        out_shape=jax.ShapeDtypeStruct((t, D_MODEL), F32),
        compiler_params=_cparams(("arbitrary",)),
        name="final",
    )(ht, ft, g_col)


def kernel(x_prompt, x_sample, state_a_k, state_a_v, state_b_k, state_b_v, g_attn, w_in, attn_sink,
           g_out_a, g_out_b, w_out, g_ffn, peer_wq, peer_k1, peer_k2, peer_u, peer_v, g_final):
    bsz, seq, _ = x_prompt.shape
    db, dt, _ = x_sample.shape
    assert w_in.shape[0] == 1 and dt == 1
    past = PAST_LEN
    lb = state_b_k.shape[2]
    assert lb == max(w for w, _ in DILATED)

    w_in_b = w_in[0].astype(BF16)
    w_out_b = w_out[0].astype(BF16)
    wq_t = peer_wq[0].reshape(D_MODEL, PEER_HEADS * PEER_DKEY).T.astype(BF16)
    k1_b = peer_k1[0].astype(BF16)
    k2_b = peer_k2[0].astype(BF16)
    u_b = peer_u[0].astype(BF16)
    vt_b = _cast_transposed(peer_v[0])

    rope_flags = jnp.concatenate([
        jnp.ones((1, WIDTH_A + N_KV_A * HEAD_DIM), F32), jnp.zeros((1, N_KV_A * HEAD_DIM), F32),
        jnp.ones((1, 2 * WIDTH_B), F32), jnp.zeros((1, WIDTH_B), F32)], axis=-1)
    cos_p, sin_p = _rope_tables(jnp.arange(seq))
    cos_s, sin_s = _rope_tables(jnp.full((db,), past))

    tm_proj, tm_merge = 1024, 512
    gf_col = g_ffn.reshape(D_MODEL, 1)
    xp = x_prompt.reshape(bsz * seq, D_MODEL)
    pp = _proj(xp, g_attn, w_in_b, cos_p, sin_p, rope_flags, tm_proj, seq // tm_proj)
    pp4 = pp.reshape(QKV_COLS // LANES, bsz, seq, LANES)
    oa_p = _attn_a(pp4, attn_sink[0])
    ob_p = _attn_b(pp4)
    ht_p, xt_p = _merge(oa_p.reshape(bsz * seq, WIDTH_A), ob_p.reshape(bsz * seq, WIDTH_B), xp,
                        g_out_a, g_out_b, w_out_b, gf_col, tm_merge)

    xs = x_sample.reshape(db, D_MODEL)
    ps = _proj(xs, g_attn, w_in_b, cos_s, sin_s, rope_flags, db, 1)

    def sample_cols(lo, hi):
        return jnp.transpose(ps[lo:hi], (1, 0, 2)).reshape(db, (hi - lo) * LANES)

    qa_s = sample_cols(COL_QA, COL_KA).reshape(db, N_HEADS_A, HEAD_DIM)
    ka_s = sample_cols(COL_KA, COL_VA).reshape(db, N_KV_A, HEAD_DIM)
    va_s = sample_cols(COL_VA, COL_QB).reshape(db, N_KV_A, HEAD_DIM)
    kb_s = sample_cols(COL_KB, COL_VB)
    vb_s = sample_cols(COL_VB, QKV_COLS // LANES)
    kt_a = jnp.transpose(state_a_k[0], (0, 2, 3, 1))
    vt_a = jnp.transpose(state_a_v[0], (0, 2, 3, 1))
    skt_b = jnp.transpose(state_b_k[0], (0, 2, 3, 1))
    svt_b = jnp.transpose(state_b_v[0], (0, 2, 3, 1))
    oa_s = _samp_a(qa_s, kt_a, vt_a, ka_s, va_s, attn_sink[0].reshape(N_KV_A, G_A, 1))
    obt_s = _samp_b(skt_b, svt_b, sample_cols(COL_QB, COL_KB).T, kb_s.T, vb_s.T)
    ht_s, xt_s = _merge(oa_s.reshape(db, WIDTH_A), obt_s.T, xs, g_out_a, g_out_b, w_out_b, gf_col, db)

    g_col = g_final.reshape(D_MODEL, 1)

    def peer_and_final(ht, xt, tm_router, tm_experts, tm_final):
        th, e1, s2, e2 = _router(xt, wq_t, k1_b, k2_b, tm_router)
        ft = _experts(xt, u_b, vt_b, th, e1, s2, e2, tm_experts)
        return _final(ht, ft, g_col, tm_final)

    y_p = peer_and_final(ht_p, xt_p, 1024, 512, 256).reshape(bsz, seq, D_MODEL)
    y_s = peer_and_final(ht_s, xt_s, db, db, db).reshape(db, 1, D_MODEL)

    def prompt_cols(lo, hi, first_row, heads):
        blk = jnp.transpose(pp4[lo:hi, :, first_row:], (1, 2, 0, 3))
        return blk.reshape(1, bsz, seq - first_row, heads, HEAD_DIM)

    rows_a = min(WINDOW_A, seq)
    ka_p = prompt_cols(COL_KA, COL_VA, seq - rows_a, N_KV_A)
    va_p = prompt_cols(COL_VA, COL_QB, seq - rows_a, N_KV_A)
    kb_p = prompt_cols(COL_KB, COL_VB, 0, N_HEADS_B)
    vb_p = prompt_cols(COL_VB, QKV_COLS // LANES, 0, N_HEADS_B)
    return (y_p, y_s, ka_p, va_p, kb_p, vb_p,
            ka_s.reshape(1, db, 1, N_KV_A, HEAD_DIM), va_s.reshape(1, db, 1, N_KV_A, HEAD_DIM),
            kb_s.reshape(1, db, 1, N_HEADS_B, HEAD_DIM), vb_s.reshape(1, db, 1, N_HEADS_B, HEAD_DIM))
```
